```python
import math
import jax, jax.numpy as jnp
from jax import lax
import numpy as np

D_MODEL = 1024
BATCH = 4
SEQ = 8192
DEPTH = 1

MIX_WIDTH = D_MODEL
ATTN_WIDTH = MIX_WIDTH // 2
SSM_WIDTH = MIX_WIDTH - ATTN_WIDTH
HEAD_DIM = 64
N_HEADS = ATTN_WIDTH // HEAD_DIM
DILATED_BRANCHES = ((128, 1), (512, 4), (2048, 16))
BLOCK = 128
SSM_GROUP = 16
N_SSM_GROUPS = SSM_WIDTH // SSM_GROUP
STATE_DIM = 64
D_FF = 2816
IN_WIDTH = 3 * ATTN_WIDTH + SSM_WIDTH
NORM_EPS = 1e-6
DT_MIN = 1e-3
DT_MAX = 1e-1

kernel_name = "hybrid_dilated_alibi_attn_s5_macaron_layer"


def rms_norm(x, g):
    xf = x.astype(jnp.float32)
    y = xf * lax.rsqrt(jnp.mean(xf * xf, axis=-1, keepdims=True) + NORM_EPS)
    return (y * g.astype(jnp.float32)).astype(x.dtype)


def swiglu(x, w_in, w_out):
    gate, up = jnp.split(x @ w_in, 2, axis=-1)
    return (jax.nn.silu(gate) * up) @ w_out


def alibi_slopes(n_heads):
    return 2.0 ** (-8.0 * jnp.arange(1, n_heads + 1, dtype=jnp.float32) / n_heads)


def dilated_window_branch(q, k, v, slopes, window, dilation):
    B, S, H, E = q.shape
    n_back = window // dilation
    L = -(-S // dilation)
    nb = -(-L // BLOCK)
    Lp = nb * BLOCK

    def to_blocks(t):
        t = jnp.pad(t, ((0, 0), (0, L * dilation - S), (0, 0), (0, 0)))
        t = t.reshape(B, L, dilation, H, E).transpose(0, 2, 1, 3, 4)
        t = jnp.pad(t, ((0, 0), (0, 0), (0, Lp - L), (0, 0), (0, 0)))
        return t.reshape(B, dilation, nb, BLOCK, H, E)

    def with_prev(t):
        prev = jnp.pad(t[:, :, :-1], ((0, 0), (0, 0), (1, 0), (0, 0), (0, 0), (0, 0)))
        return jnp.concatenate([prev, t], axis=3)

    def from_blocks(t):
        tail = t.shape[4:]
        t = t.reshape((B, dilation, Lp) + tail)[:, :, :L]
        t = jnp.moveaxis(t, 1, 2).reshape((B, L * dilation) + tail)
        return t[:, :S]

    qb, kb, vb = to_blocks(q), to_blocks(k), to_blocks(v)
    kk, vv = with_prev(kb), with_prev(vb)
    s = jnp.einsum('brnqhe,brnkhe->brnhqk', qb, kk) * (HEAD_DIM ** -0.5)

    qi = jnp.arange(BLOCK)[:, None]
    ci = jnp.arange(2 * BLOCK)[None, :]
    steps = BLOCK + qi - ci
    key_pos = (jnp.arange(nb)[:, None, None] - 1) * BLOCK + ci[None]
    valid = ((steps >= 0) & (steps <= n_back))[None] & (key_pos >= 0)
    dist = (steps * dilation).astype(jnp.float32)
    bias = -slopes[:, None, None] * dist[None]
    s = jnp.where(valid[None, None, :, None], s + bias, -jnp.inf)

    m = jnp.max(s, axis=-1, keepdims=True)
    p = jnp.exp(s - m)
    denom = jnp.sum(p, axis=-1, keepdims=True)
    o = jnp.einsum('brnhqk,brnkhe->brnqhe', p, vv)
    o = o * jnp.swapaxes(1.0 / denom[..., 0], -1, -2)[..., None]
    lse = jnp.swapaxes((m + jnp.log(denom))[..., 0], -1, -2)
    return from_blocks(o), from_blocks(lse)


def dilated_attention(q, k, v):
    B, S, _ = q.shape
    q, k, v = (t.astype(jnp.float32).reshape(B, S, N_HEADS, HEAD_DIM) for t in (q, k, v))
    slopes = alibi_slopes(N_HEADS)
    outs, lses = [], []
    for window, dilation in DILATED_BRANCHES:
        o, l = dilated_window_branch(q, k, v, slopes, window, dilation)
        outs.append(o)
        lses.append(l)
    w = jax.nn.softmax(jnp.stack(lses, axis=-1), axis=-1)
    o = jnp.einsum('bshn,nbshe->bshe', w, jnp.stack(outs, axis=0))
    return o.reshape(B, S, ATTN_WIDTH)


def s5_mixer(u, a_re, a_im, log_dt, b_re, b_im, c_re, c_im, d_skip, w_glu, b_glu):
    B, S, _ = u.shape
    f32 = jnp.float32
    uf = u.astype(f32).reshape(B, S, N_SSM_GROUPS, SSM_GROUP)
    dt = jnp.exp(log_dt.astype(f32))[:, None]
    a = lax.complex(a_re.astype(f32), a_im.astype(f32))
    a_bar = jnp.exp(dt * a)
    b = lax.complex(b_re.astype(f32), b_im.astype(f32))
    b_bar = ((a_bar - 1.0) / a)[..., None] * b
    bu = jnp.einsum('bsgc,gpc->bsgp', uf.astype(jnp.complex64), b_bar)
    a_seq = jnp.broadcast_to(a_bar, bu.shape)

    def combine(left, right):
        a_l, x_l = left
        a_r, x_r = right
        return a_r * a_l, a_r * x_l + x_r

    _, states = lax.associative_scan(combine, (a_seq, bu), axis=1)
    c = lax.complex(c_re.astype(f32), c_im.astype(f32))
    y = jnp.real(jnp.einsum('bsgp,gcp->bsgc', states, c))
    y = y + d_skip.astype(f32).reshape(N_SSM_GROUPS, SSM_GROUP) * uf
    y = jax.nn.gelu(y.reshape(B, S, SSM_WIDTH))
    return y * jax.nn.sigmoid(y @ w_glu.astype(f32) + b_glu.astype(f32))


def setup_inputs(seed: int = 0) -> dict:
    key = jax.random.key(seed)
    ks = jax.random.split(key, 24)
    f32 = jnp.float32
    L = DEPTH

    def nrm(k, shape, scale):
        return jax.random.normal(k, shape, f32) * scale

    def gain(k):
        return 1.0 + 0.05 * jax.random.normal(k, (L, D_MODEL), f32)

    n_idx = jnp.arange(STATE_DIM, dtype=f32)
    a_re = -0.5 + 0.01 * jax.random.normal(ks[9], (L, N_SSM_GROUPS, STATE_DIM), f32)
    a_im = math.pi * n_idx + 0.01 * jax.random.normal(ks[10], (L, N_SSM_GROUPS, STATE_DIM), f32)
    log_dt = jax.random.uniform(ks[11], (L, N_SSM_GROUPS), f32,
                                math.log(DT_MIN), math.log(DT_MAX))
    return {
        "x": jax.random.normal(ks[0], (BATCH, SEQ, D_MODEL), f32),
        "ffn1_pre_g": gain(ks[1]),
        "ffn1_w_in": nrm(ks[2], (L, D_MODEL, 2 * D_FF), D_MODEL ** -0.5),
        "ffn1_w_out": nrm(ks[3], (L, D_FF, D_MODEL), D_FF ** -0.5),
        "ffn1_post_g": gain(ks[4]),
        "mix_pre_g": gain(ks[5]),
        "w_mix_in": nrm(ks[6], (L, D_MODEL, IN_WIDTH), D_MODEL ** -0.5),
        "a_re": a_re,
        "a_im": a_im,
        "log_dt": log_dt,
        "b_re": nrm(ks[12], (L, N_SSM_GROUPS, STATE_DIM, SSM_GROUP), (2 * SSM_GROUP) ** -0.5),
        "b_im": nrm(ks[13], (L, N_SSM_GROUPS, STATE_DIM, SSM_GROUP), (2 * SSM_GROUP) ** -0.5),
        "c_re": nrm(ks[14], (L, N_SSM_GROUPS, SSM_GROUP, STATE_DIM), (2 * STATE_DIM) ** -0.5),
        "c_im": nrm(ks[15], (L, N_SSM_GROUPS, SSM_GROUP, STATE_DIM), (2 * STATE_DIM) ** -0.5),
        "d_skip": nrm(ks[16], (L, SSM_WIDTH), 1.0),
        "w_glu": nrm(ks[17], (L, SSM_WIDTH, SSM_WIDTH), SSM_WIDTH ** -0.5),
        "b_glu": nrm(ks[18], (L, SSM_WIDTH), 0.01),
        "w_mix_out": nrm(ks[19], (L, MIX_WIDTH, D_MODEL), MIX_WIDTH ** -0.5),
        "mix_post_g": gain(ks[20]),
        "ffn2_pre_g": gain(ks[21]),
        "ffn2_w_in": nrm(ks[22], (L, D_MODEL, 2 * D_FF), D_MODEL ** -0.5),
        "ffn2_w_out": nrm(ks[23], (L, D_FF, D_MODEL), D_FF ** -0.5),
        "ffn2_post_g": gain(ks[7]),
    }


def reference(x, ffn1_pre_g, ffn1_w_in, ffn1_w_out, ffn1_post_g, mix_pre_g, w_mix_in,
              a_re, a_im, log_dt, b_re, b_im, c_re, c_im, d_skip, w_glu, b_glu,
              w_mix_out, mix_post_g, ffn2_pre_g, ffn2_w_in, ffn2_w_out, ffn2_post_g):
    for l in range(DEPTH):
        h = rms_norm(x, ffn1_pre_g[l])
        x = x + 0.5 * rms_norm(swiglu(h, ffn1_w_in[l], ffn1_w_out[l]), ffn1_post_g[l])
        h = rms_norm(x, mix_pre_g[l])
        proj = h @ w_mix_in[l]
        q, k, v, u = jnp.split(proj, [ATTN_WIDTH, 2 * ATTN_WIDTH, 3 * ATTN_WIDTH], axis=-1)
        attn = dilated_attention(q, k, v).astype(x.dtype)
        ssm = s5_mixer(u, a_re[l], a_im[l], log_dt[l], b_re[l], b_im[l], c_re[l], c_im[l],
                       d_skip[l], w_glu[l], b_glu[l]).astype(x.dtype)
        mixed = jnp.concatenate([attn, ssm], axis=-1) @ w_mix_out[l]
        x = x + rms_norm(mixed, mix_post_g[l])
        h = rms_norm(x, ffn2_pre_g[l])
        x = x + 0.5 * rms_norm(swiglu(h, ffn2_w_in[l], ffn2_w_out[l]), ffn2_post_g[l])
    return x
```

```python
import functools

import jax
import jax.numpy as jnp
from jax import lax
from jax.experimental import pallas as pl
from jax.experimental.pallas import tpu as pltpu

F32 = jnp.float32
BF16 = jnp.bfloat16

D_MODEL = 1024
ATTN_WIDTH = 512
SSM_WIDTH = 512
HEAD_DIM = 64
N_HEADS = 8
DILATIONS = (1, 4, 16)
WINDOW_STEPS = 128
QBLK = 128
SSM_GROUP = 16
N_SSM_GROUPS = 32
STATE_DIM = 64
D_FF = 2816
NORM_EPS = 1e-6

TOKEN_TILE = 512
FF_CHUNK = 256
SSM_CHUNK = 64
ATTN_TILE = 512
MASK_VALUE = -1e30
LANES = 128
VMEM_LIMIT_BYTES = 56 * 1024 * 1024


def _rms(x, g):
    return x * lax.rsqrt(jnp.mean(x * x, axis=-1, keepdims=True) + NORM_EPS) * g


def _swiglu_ffn(x, pre_g, wg_ref, wu_ref, wo_ref, post_g):
    h = _rms(x, pre_g).astype(BF16)
    acc = jnp.zeros((x.shape[0], D_MODEL), F32)
    for j in range(D_FF // FF_CHUNK):
        sl = slice(j * FF_CHUNK, (j + 1) * FF_CHUNK)
        gate = jnp.dot(h, wg_ref[:, sl], preferred_element_type=F32)
        up = jnp.dot(h, wu_ref[:, sl], preferred_element_type=F32)
        act = (gate * jax.nn.sigmoid(gate) * up).astype(BF16)
        acc = acc + jnp.dot(act, wo_ref[sl, :], preferred_element_type=F32)
    return x + 0.5 * _rms(acc, post_g)


def _ffn1_proj_kernel(x_ref, g1_ref, wg_ref, wu_ref, wo_ref, p1_ref, gm_ref, wm_ref,
                      x1_ref, q1_ref, k1_ref, v1_ref, q4_ref, k4_ref, v4_ref,
                      q16_ref, k16_ref, v16_ref, u_ref, proj_scr):
    x1 = _swiglu_ffn(x_ref[0], g1_ref[...], wg_ref, wu_ref, wo_ref, p1_ref[...])
    x1_ref[0] = x1
    h = _rms(x1, gm_ref[...]).astype(BF16)
    proj = jnp.dot(h, wm_ref[...], preferred_element_type=F32)
    lanes_per = ATTN_WIDTH // LANES
    for cb in range(3 * lanes_per):
        proj_scr[cb] = proj[:, cb * LANES:(cb + 1) * LANES]
    outs = ((q1_ref, q4_ref, q16_ref), (k1_ref, k4_ref, k16_ref), (v1_ref, v4_ref, v16_ref))
    for t, (o1, o4, o16) in enumerate(outs):
        o1[0] = proj[:, ATTN_WIDTH * t:ATTN_WIDTH * (t + 1)].astype(BF16)
        for d, o in ((4, o4), (16, o16)):
            for r in range(d):
                for c in range(lanes_per):
                    o[0, r, :, c * LANES:(c + 1) * LANES] = proj_scr[
                        t * lanes_per + c, pl.ds(r, TOKEN_TILE // d, stride=d), :].astype(BF16)
    u_ref[0] = proj[:, 3 * ATTN_WIDTH:].astype(BF16)


def _const_spec(shape):
    return pl.BlockSpec(shape, lambda *_: (0,) * len(shape), pipeline_mode=pl.Buffered(1))


def _ffn1_proj(x, g1, wg, wu, wo, p1, gm, wm):
    B, S, _ = x.shape
    tm = TOKEN_TILE
    tok = lambda w: pl.BlockSpec((1, tm, w), lambda b, i: (b, i, 0))
    res = lambda d: pl.BlockSpec((1, d, tm // d, ATTN_WIDTH), lambda b, i: (b, 0, i, 0))
    nat = jax.ShapeDtypeStruct((B, S, ATTN_WIDTH), BF16)
    r4 = jax.ShapeDtypeStruct((B, 4, S // 4, ATTN_WIDTH), BF16)
    r16 = jax.ShapeDtypeStruct((B, 16, S // 16, ATTN_WIDTH), BF16)
    return pl.pallas_call(
        _ffn1_proj_kernel,
        grid=(B, S // tm),
        in_specs=[tok(D_MODEL), _const_spec((1, D_MODEL)),
                  _const_spec((D_MODEL, D_FF)), _const_spec((D_MODEL, D_FF)), _const_spec((D_FF, D_MODEL)),
                  _const_spec((1, D_MODEL)), _const_spec((1, D_MODEL)), _const_spec((D_MODEL, 4 * ATTN_WIDTH))],
        out_specs=[tok(D_MODEL)] + [tok(ATTN_WIDTH)] * 3 + [res(4)] * 3 + [res(16)] * 3 + [tok(SSM_WIDTH)],
        out_shape=[jax.ShapeDtypeStruct((B, S, D_MODEL), F32)] + [nat] * 3 + [r4] * 3 + [r16] * 3 + [nat],
        scratch_shapes=[pltpu.VMEM((3 * ATTN_WIDTH // LANES, tm, LANES), F32)],
        compiler_params=pltpu.CompilerParams(
            dimension_semantics=("arbitrary", "arbitrary"), vmem_limit_bytes=VMEM_LIMIT_BYTES),
        name="ffn1_proj",
    )(x, g1, wg, wu, wo, p1, gm, wm)


def _attn_kernel(q_ref, kc_ref, kp_ref, vc_ref, vp_ref, bias_ref, o_ref, lse_ref, k_scr, v_scr,
                 *, tiles_per_seq):
    first = (lax.rem(pl.program_id(0), tiles_per_seq) == 0).astype(jnp.int32)
    k_scr[0:QBLK] = kp_ref[...]
    k_scr[QBLK:] = kc_ref[...]
    v_scr[0:QBLK] = vp_ref[...]
    v_scr[QBLK:] = vc_ref[...]
    lane = lax.broadcasted_iota(jnp.int32, (1, 2 * HEAD_DIM), 1)
    head_mask = ((lane < HEAD_DIM).astype(BF16), (lane >= HEAD_DIM).astype(BF16))
    lane_f = lax.broadcasted_iota(jnp.int32, (QBLK, 2 * HEAD_DIM), 1)
    low_half = lane_f < HEAD_DIM

    def block(j, carry):
        rows = pl.ds(pl.multiple_of(j * QBLK, QBLK), QBLK)
        krows = pl.ds(pl.multiple_of(j * QBLK, QBLK), 2 * QBLK)
        table = jnp.where(j == 0, first, 0)
        lse_tile = jnp.zeros((QBLK, 2 * HEAD_DIM), F32)
        for hp in range(N_HEADS // 2):
            cs = slice(2 * HEAD_DIM * hp, 2 * HEAD_DIM * (hp + 1))
            q = q_ref[rows, cs]
            kk = k_scr[krows, cs]
            vv = v_scr[krows, cs]
            halves = []
            for e in range(2):
                h = 2 * hp + e
                s = lax.dot_general(q * head_mask[e], kk, (((1,), (1,)), ((), ())),
                                    preferred_element_type=F32)
                s = s * (HEAD_DIM ** -0.5) + bias_ref[table, h]
                m = jnp.max(s, axis=-1, keepdims=True)
                p = jnp.exp(s - m)
                denom = jnp.sum(p, axis=-1, keepdims=True)
                pv = jnp.dot(p.astype(BF16), vv, preferred_element_type=F32)
                halves.append(pv * (1.0 / denom))
                lse_tile = jnp.where(lane_f == h, m + jnp.log(denom), lse_tile)
            o_ref[rows, cs] = jnp.where(low_half, halves[0], halves[1]).astype(BF16)
        lse_ref[rows, :] = lse_tile
        return carry

    lax.fori_loop(0, ATTN_TILE // QBLK, block, 0)


def _attention_branch(q, k, v, bias, seq_len):
    rows = q.shape[0]
    ratio = ATTN_TILE // QBLK
    cur = pl.BlockSpec((ATTN_TILE, ATTN_WIDTH), lambda i: (i, 0))
    prev = pl.BlockSpec((QBLK, ATTN_WIDTH), lambda i: (jnp.maximum(i * ratio - 1, 0), 0))
    return pl.pallas_call(
        functools.partial(_attn_kernel, tiles_per_seq=seq_len // ATTN_TILE),
        grid=(rows // ATTN_TILE,),
        in_specs=[cur, cur, prev, cur, prev,
                  pl.BlockSpec(bias.shape, lambda i: (0, 0, 0, 0))],
        out_specs=[cur, pl.BlockSpec((ATTN_TILE, 2 * HEAD_DIM), lambda i: (i, 0))],
        out_shape=[jax.ShapeDtypeStruct((rows, ATTN_WIDTH), BF16),
                   jax.ShapeDtypeStruct((rows, 2 * HEAD_DIM), F32)],
        scratch_shapes=[pltpu.VMEM((ATTN_TILE + QBLK, ATTN_WIDTH), BF16)] * 2,
        compiler_params=pltpu.CompilerParams(dimension_semantics=("arbitrary",)),
        name=f"attn_s{seq_len}",
    )(q, k, k, v, v, bias)


def _alibi_bias_tables(dilation):
    slopes = 2.0 ** (-8.0 * jnp.arange(1, N_HEADS + 1, dtype=F32) / N_HEADS)
    qi = jnp.arange(QBLK)[:, None]
    ci = jnp.arange(2 * QBLK)[None, :]
    steps = QBLK + qi - ci
    band = (steps >= 0) & (steps <= WINDOW_STEPS)
    dist = (steps * dilation).astype(F32)
    bias = -slopes[:, None, None] * dist[None]
    normal = jnp.where(band[None], bias, MASK_VALUE)
    first = jnp.where((band & (ci >= QBLK))[None], bias, MASK_VALUE)
    return jnp.stack([normal, first], axis=0)


def _ssm_kernel(u_ref, a1_ref, cw_ref, sc_ref, y_ref):
    n_rows = SSM_GROUP * SSM_CHUNK
    n_cols = u_ref.shape[-1]
    chunks_per_seq = 128
    u = u_ref[...].reshape(n_rows, n_cols)
    y1 = jnp.dot(a1_ref[0], u, preferred_element_type=F32)
    xr = y1[n_rows:n_rows + STATE_DIM]
    xi = y1[n_rows + STATE_DIM:]
    kidx = lax.broadcasted_iota(jnp.int32, (STATE_DIM, n_cols), 1) & (chunks_per_seq - 1)
    for i in range(7):
        sh = 1 << i
        rr = pltpu.roll(xr, sh, axis=1)
        ri = pltpu.roll(xi, sh, axis=1)
        ar = sc_ref[0, 0:STATE_DIM, i:i + 1]
        ai = sc_ref[0, STATE_DIM:, i:i + 1]
        ok = kidx >= sh
        xr, xi = (xr + jnp.where(ok, ar * rr - ai * ri, 0.0),
                  xi + jnp.where(ok, ar * ri + ai * rr, 0.0))
    ok = kidx >= 1
    hr = jnp.where(ok, pltpu.roll(xr, 1, axis=1), 0.0)
    hi = jnp.where(ok, pltpu.roll(xi, 1, axis=1), 0.0)
    h = jnp.concatenate([hr, hi], axis=0).astype(BF16)
    y = y1[:n_rows] + jnp.dot(cw_ref[0], h, preferred_element_type=F32)
    y_ref[...] = jax.nn.gelu(y).astype(BF16).reshape(SSM_GROUP, SSM_CHUNK, n_cols)


def _ssm(ut, a1, cw, sc):
    n_cols = ut.shape[-1]
    blk = pl.BlockSpec((SSM_GROUP, SSM_CHUNK, n_cols), lambda g: (g, 0, 0))
    per_group = lambda a: pl.BlockSpec((1,) + a.shape[1:], lambda g: (g, 0, 0))
    return pl.pallas_call(
        _ssm_kernel,
        grid=(N_SSM_GROUPS,),
        in_specs=[blk, per_group(a1), per_group(cw), per_group(sc)],
        out_specs=blk,
        out_shape=jax.ShapeDtypeStruct(ut.shape, BF16),
        compiler_params=pltpu.CompilerParams(dimension_semantics=("arbitrary",)),
        name="ssm",
    )(ut, a1, cw, sc)


def _ssm_operators(a_re, a_im, log_dt, b_re, b_im, c_re, c_im, d_skip):
    T = SSM_CHUNK
    hi = lax.Precision.HIGHEST
    dt = jnp.exp(log_dt.astype(F32))[:, None]
    a = lax.complex(a_re.astype(F32), a_im.astype(F32))
    a_bar = jnp.exp(dt * a)
    b_bar = ((a_bar - 1.0) / a)[..., None] * lax.complex(b_re.astype(F32), b_im.astype(F32))
    c = lax.complex(c_re.astype(F32), c_im.astype(F32))
    n = jnp.arange(T + 1, dtype=F32)
    pw = jnp.exp(n[None, :, None] * (dt * a)[:, None, :])
    kern = jnp.real(jnp.einsum('gcp,glp,gpd->glcd', c, pw[:, :T], b_bar, precision=hi))
    d = d_skip.astype(F32).reshape(N_SSM_GROUPS, SSM_GROUP)
    kern = kern.at[:, 0].add(d[:, :, None] * jnp.eye(SSM_GROUP, dtype=F32)[None])
    lag = jnp.arange(T)[:, None] - jnp.arange(T)[None, :]
    toep = jnp.where((lag >= 0)[None, :, :, None, None], kern[:, jnp.clip(lag, 0, T - 1)], 0.0)
    toep = toep.transpose(0, 3, 1, 4, 2).reshape(N_SSM_GROUPS, SSM_GROUP * T, SSM_GROUP * T)
    bs = pw[:, T - 1::-1][:, :T]
    bs = bs.transpose(0, 2, 1)[:, :, None, :] * b_bar[:, :, :, None]
    bs = bs.reshape(N_SSM_GROUPS, STATE_DIM, SSM_GROUP * T)
    a1 = jnp.concatenate([toep, jnp.real(bs), jnp.imag(bs)], axis=1).astype(BF16)
    cwc = c[:, :, None, :] * pw[:, None, 1:, :]
    cwc = cwc.reshape(N_SSM_GROUPS, SSM_GROUP * T, STATE_DIM)
    cw = jnp.concatenate([jnp.real(cwc), -jnp.imag(cwc)], axis=2).astype(BF16)
    e = (T * (2.0 ** jnp.arange(7, dtype=F32)))[None, None, :]
    sp = jnp.exp(e * (dt * a)[:, :, None])
    sc = jnp.concatenate([jnp.real(sp), jnp.imag(sp)], axis=1)
    sc = jnp.pad(sc, ((0, 0), (0, 0), (0, 128 - 7)))
    return a1, cw, sc


def _mix_kernel(x1_ref, o1_ref, o4_ref, o16_ref, l1_ref, l4_ref, l16_ref, z_ref,
                wglu_ref, bglu_ref, wout_ref, gpost_ref, x2_ref, o_scr, l_scr):
    tm = TOKEN_TILE
    for n, (d, o_ref, l_ref) in enumerate(((4, o4_ref, l4_ref), (16, o16_ref, l16_ref))):
        for r in range(d):
            for c in range(ATTN_WIDTH // LANES):
                o_scr[n, c, pl.ds(r, tm // d, stride=d), :] = o_ref[0, r, :, c * LANES:(c + 1) * LANES].astype(F32)
            l_scr[n, pl.ds(r, tm // d, stride=d), :] = l_ref[0, r]
    lses = (l1_ref[0], l_scr[0], l_scr[1])
    m = jnp.maximum(jnp.maximum(lses[0], lses[1]), lses[2])
    es = [jnp.exp(l - m) for l in lses]
    inv = 1.0 / (es[0] + es[1] + es[2])
    ws = [e * inv for e in es]
    low_half = lax.broadcasted_iota(jnp.int32, (tm, 2 * HEAD_DIM), 1) < HEAD_DIM
    pairs = []
    for hp in range(N_HEADS // 2):
        cs = slice(2 * HEAD_DIM * hp, 2 * HEAD_DIM * (hp + 1))
        os_ = (o1_ref[0, :, cs].astype(F32), o_scr[0, hp], o_scr[1, hp])
        acc = jnp.zeros((tm, 2 * HEAD_DIM), F32)
        for w, o in zip(ws, os_):
            wexp = jnp.where(low_half, w[:, 2 * hp:2 * hp + 1], w[:, 2 * hp + 1:2 * hp + 2])
            acc = acc + wexp * o
        pairs.append(acc)
    attn = jnp.concatenate(pairs, axis=-1).astype(BF16)
    z = z_ref[0]
    gate = jax.nn.sigmoid(jnp.dot(z, wglu_ref[...], preferred_element_type=F32) + bglu_ref[...])
    ssm = (z.astype(F32) * gate).astype(BF16)
    mixed = (jnp.dot(attn, wout_ref[0:ATTN_WIDTH, :], preferred_element_type=F32)
             + jnp.dot(ssm, wout_ref[ATTN_WIDTH:, :], preferred_element_type=F32))
    x2_ref[0] = x1_ref[0] + _rms(mixed, gpost_ref[...])


def _mix(x1, o1, o4, o16, l1, l4, l16, z, wglu, bglu, wout, gpost):
    B, S, _ = x1.shape
    tm = TOKEN_TILE
    tok = lambda w: pl.BlockSpec((1, tm, w), lambda b, i: (b, i, 0))
    res = lambda d, w: pl.BlockSpec((1, d, tm // d, w), lambda b, i: (b, 0, i, 0))
    return pl.pallas_call(
        _mix_kernel,
        grid=(B, S // tm),
        in_specs=[tok(D_MODEL), tok(ATTN_WIDTH), res(4, ATTN_WIDTH), res(16, ATTN_WIDTH),
                  tok(2 * HEAD_DIM), res(4, 2 * HEAD_DIM), res(16, 2 * HEAD_DIM), tok(SSM_WIDTH),
                  _const_spec((SSM_WIDTH, SSM_WIDTH)), _const_spec((1, SSM_WIDTH)),
                  _const_spec((D_MODEL, D_MODEL)), _const_spec((1, D_MODEL))],
        out_specs=tok(D_MODEL),
        out_shape=jax.ShapeDtypeStruct((B, S, D_MODEL), F32),
        scratch_shapes=[pltpu.VMEM((2, ATTN_WIDTH // LANES, tm, LANES), F32),
                        pltpu.VMEM((2, tm, 2 * HEAD_DIM), F32)],
        compiler_params=pltpu.CompilerParams(
            dimension_semantics=("arbitrary", "arbitrary"), vmem_limit_bytes=VMEM_LIMIT_BYTES),
        name="mix",
    )(x1, o1, o4, o16, l1, l4, l16, z, wglu, bglu, wout, gpost)


def _ffn_kernel(x_ref, gpre_ref, wg_ref, wu_ref, wo_ref, gpost_ref, y_ref):
    y_ref[0] = _swiglu_ffn(x_ref[0], gpre_ref[...], wg_ref, wu_ref, wo_ref, gpost_ref[...])


def _ffn(x, gpre, wg, wu, wo, gpost):
    B, S, _ = x.shape
    tok = pl.BlockSpec((1, TOKEN_TILE, D_MODEL), lambda b, i: (b, i, 0))
    return pl.pallas_call(
        _ffn_kernel,
        grid=(B, S // TOKEN_TILE),
        in_specs=[tok, _const_spec((1, D_MODEL)),
                  _const_spec((D_MODEL, D_FF)), _const_spec((D_MODEL, D_FF)), _const_spec((D_FF, D_MODEL)),
                  _const_spec((1, D_MODEL))],
        out_specs=tok,
        out_shape=jax.ShapeDtypeStruct(x.shape, F32),
        compiler_params=pltpu.CompilerParams(
            dimension_semantics=("arbitrary", "arbitrary"), vmem_limit_bytes=VMEM_LIMIT_BYTES),
        name="ffn2",
    )(x, gpre, wg, wu, wo, gpost)


def _split_ffn_weights(w_in, w_out):
    return w_in[:, :D_FF].astype(BF16), w_in[:, D_FF:].astype(BF16), w_out.astype(BF16)


def _row(v):
    return v.astype(F32).reshape(1, -1)


def kernel(x, ffn1_pre_g, ffn1_w_in, ffn1_w_out, ffn1_post_g, mix_pre_g, w_mix_in, a_re, a_im, log_dt, b_re, b_im, c_re, c_im, d_skip, w_glu, b_glu, w_mix_out, mix_post_g, ffn2_pre_g, ffn2_w_in, ffn2_w_out, ffn2_post_g):
    B, S, _ = x.shape
    depth = ffn1_pre_g.shape[0]
    n_chunks = S // SSM_CHUNK
    for l in range(depth):
        wg1, wu1, wo1 = _split_ffn_weights(ffn1_w_in[l], ffn1_w_out[l])
        wg2, wu2, wo2 = _split_ffn_weights(ffn2_w_in[l], ffn2_w_out[l])
        (x1, q1, k1, v1, q4, k4, v4, q16, k16, v16, u) = _ffn1_proj(
            x, _row(ffn1_pre_g[l]), wg1, wu1, wo1, _row(ffn1_post_g[l]), _row(mix_pre_g[l]),
            w_mix_in[l].astype(BF16))

        outs = []
        for d, (q, k, v) in zip(DILATIONS, ((q1, k1, v1), (q4, k4, v4), (q16, k16, v16))):
            flat = lambda t: t.reshape(B * S, ATTN_WIDTH)
            o, lse = _attention_branch(flat(q), flat(k), flat(v), _alibi_bias_tables(d), S // d)
            outs.append((o, lse))
        o1 = outs[0][0].reshape(B, S, ATTN_WIDTH)
        l1 = outs[0][1].reshape(B, S, 2 * HEAD_DIM)
        o4 = outs[1][0].reshape(B, 4, S // 4, ATTN_WIDTH)
        l4 = outs[1][1].reshape(B, 4, S // 4, 2 * HEAD_DIM)
        o16 = outs[2][0].reshape(B, 16, S // 16, ATTN_WIDTH)
        l16 = outs[2][1].reshape(B, 16, S // 16, 2 * HEAD_DIM)

        a1, cw, sc = _ssm_operators(a_re[l], a_im[l], log_dt[l], b_re[l], b_im[l], c_re[l], c_im[l], d_skip[l])
        ut = u.reshape(B, n_chunks, SSM_CHUNK, SSM_WIDTH).transpose(3, 2, 0, 1)
        ut = ut.reshape(SSM_WIDTH, SSM_CHUNK, B * n_chunks)
        zt = _ssm(ut, a1, cw, sc)
        z = zt.reshape(SSM_WIDTH, SSM_CHUNK, B, n_chunks).transpose(2, 3, 1, 0).reshape(B, S, SSM_WIDTH)

        x2 = _mix(x1, o1, o4, o16, l1, l4, l16, z, w_glu[l].astype(BF16), _row(b_glu[l]),
                  w_mix_out[l].astype(BF16), _row(mix_post_g[l]))
        x = _ffn(x2, _row(ffn2_pre_g[l]), wg2, wu2, wo2, _row(ffn2_post_g[l]))
    return x
```

```python
import functools

import jax
import jax.numpy as jnp
from jax import lax
from jax.experimental import pallas as pl
from jax.experimental.pallas import tpu as pltpu

F32 = jnp.float32
BF16 = jnp.bfloat16

D_MODEL = 1024
ATTN_WIDTH = 512
SSM_WIDTH = 512
HEAD_DIM = 64
N_HEADS = 8
DILATIONS = (1, 4, 16)
WINDOW_STEPS = 128
QBLK = 128
SSM_GROUP = 16
N_SSM_GROUPS = 32
STATE_DIM = 64
D_FF = 2816
NORM_EPS = 1e-6

TOKEN_TILE = 512
FF_CHUNK = 256
SSM_CHUNK = 64
ATTN_TILE = 512
MASK_VALUE = -1e30
LANES = 128
VMEM_LIMIT_BYTES = 56 * 1024 * 1024


def _rms(x, g):
    return x * lax.rsqrt(jnp.mean(x * x, axis=-1, keepdims=True) + NORM_EPS) * g


def _swiglu_ffn(x, pre_g, wg_ref, wu_ref, wo_ref, post_g):
    h = _rms(x, pre_g).astype(BF16)
    acc = jnp.zeros((x.shape[0], D_MODEL), F32)
    for j in range(D_FF // FF_CHUNK):
        sl = slice(j * FF_CHUNK, (j + 1) * FF_CHUNK)
        gate = jnp.dot(h, wg_ref[:, sl], preferred_element_type=F32)
        up = jnp.dot(h, wu_ref[:, sl], preferred_element_type=F32)
        act = (gate * jax.nn.sigmoid(gate) * up).astype(BF16)
        acc = acc + jnp.dot(act, wo_ref[sl, :], preferred_element_type=F32)
    return x + 0.5 * _rms(acc, post_g)


def _ffn1_proj_kernel(x_ref, g1_ref, wg_ref, wu_ref, wo_ref, p1_ref, gm_ref, wm_ref,
                      x1_ref, q1_ref, k1_ref, v1_ref, q4_ref, k4_ref, v4_ref,
                      q16_ref, k16_ref, v16_ref, u_ref, proj_scr):
    x1 = _swiglu_ffn(x_ref[0], g1_ref[...], wg_ref, wu_ref, wo_ref, p1_ref[...])
    x1_ref[0] = x1
    h = _rms(x1, gm_ref[...]).astype(BF16)
    proj = jnp.dot(h, wm_ref[...], preferred_element_type=F32)
    lanes_per = ATTN_WIDTH // LANES
    for cb in range(3 * lanes_per):
        proj_scr[cb] = proj[:, cb * LANES:(cb + 1) * LANES]
    outs = ((q1_ref, q4_ref, q16_ref), (k1_ref, k4_ref, k16_ref), (v1_ref, v4_ref, v16_ref))
    for t, (o1, o4, o16) in enumerate(outs):
        o1[0] = proj[:, ATTN_WIDTH * t:ATTN_WIDTH * (t + 1)].astype(BF16)
        for d, o in ((4, o4), (16, o16)):
            for r in range(d):
                for c in range(lanes_per):
                    o[0, r, :, c * LANES:(c + 1) * LANES] = proj_scr[
                        t * lanes_per + c, pl.ds(r, TOKEN_TILE // d, stride=d), :].astype(BF16)
    u_ref[0] = proj[:, 3 * ATTN_WIDTH:].astype(BF16)


def _const_spec(shape):
    return pl.BlockSpec(shape, lambda *_: (0,) * len(shape), pipeline_mode=pl.Buffered(1))


def _ffn1_proj(x, g1, wg, wu, wo, p1, gm, wm):
    B, S, _ = x.shape
    tm = TOKEN_TILE
    tok = lambda w: pl.BlockSpec((1, tm, w), lambda b, i: (b, i, 0))
    res = lambda d: pl.BlockSpec((1, d, tm // d, ATTN_WIDTH), lambda b, i: (b, 0, i, 0))
    nat = jax.ShapeDtypeStruct((B, S, ATTN_WIDTH), BF16)
    r4 = jax.ShapeDtypeStruct((B, 4, S // 4, ATTN_WIDTH), BF16)
    r16 = jax.ShapeDtypeStruct((B, 16, S // 16, ATTN_WIDTH), BF16)
    return pl.pallas_call(
        _ffn1_proj_kernel,
        grid=(B, S // tm),
        in_specs=[tok(D_MODEL), _const_spec((1, D_MODEL)),
                  _const_spec((D_MODEL, D_FF)), _const_spec((D_MODEL, D_FF)), _const_spec((D_FF, D_MODEL)),
                  _const_spec((1, D_MODEL)), _const_spec((1, D_MODEL)), _const_spec((D_MODEL, 4 * ATTN_WIDTH))],
        out_specs=[tok(D_MODEL)] + [tok(ATTN_WIDTH)] * 3 + [res(4)] * 3 + [res(16)] * 3 + [tok(SSM_WIDTH)],
        out_shape=[jax.ShapeDtypeStruct((B, S, D_MODEL), F32)] + [nat] * 3 + [r4] * 3 + [r16] * 3 + [nat],
        scratch_shapes=[pltpu.VMEM((3 * ATTN_WIDTH // LANES, tm, LANES), F32)],
        compiler_params=pltpu.CompilerParams(
            dimension_semantics=("arbitrary", "arbitrary"), vmem_limit_bytes=VMEM_LIMIT_BYTES),
        name="ffn1_proj",
    )(x, g1, wg, wu, wo, p1, gm, wm)


def _attn_kernel(q_ref, kc_ref, kp_ref, vc_ref, vp_ref, o_ref, lse_ref, k_scr, v_scr, bias_ref,
                 *, tiles_per_seq, dilation):
    @pl.when(pl.program_id(0) == 0)
    def _build_bias_tables():
        qi = lax.broadcasted_iota(jnp.int32, (QBLK, 2 * QBLK), 0)
        ci = lax.broadcasted_iota(jnp.int32, (QBLK, 2 * QBLK), 1)
        steps = QBLK + qi - ci
        dist = (steps * dilation).astype(F32)
        band = jnp.where(steps >= 0, jnp.where(steps <= WINDOW_STEPS, 1, 0), 0)
        band_first = jnp.where(ci >= QBLK, band, 0)
        for h in range(N_HEADS):
            bias = -(2.0 ** (-8.0 * (h + 1) / N_HEADS)) * dist
            bias_ref[0, h] = jnp.where(band == 1, bias, MASK_VALUE)
            bias_ref[1, h] = jnp.where(band_first == 1, bias, MASK_VALUE)

    first = (lax.rem(pl.program_id(0), tiles_per_seq) == 0).astype(jnp.int32)
    k_scr[0:QBLK] = kp_ref[...]
    k_scr[QBLK:] = kc_ref[...]
    v_scr[0:QBLK] = vp_ref[...]
    v_scr[QBLK:] = vc_ref[...]
    lane = lax.broadcasted_iota(jnp.int32, (1, 2 * HEAD_DIM), 1)
    head_mask = ((lane < HEAD_DIM).astype(BF16), (lane >= HEAD_DIM).astype(BF16))
    lane_f = lax.broadcasted_iota(jnp.int32, (QBLK, 2 * HEAD_DIM), 1)
    low_half = lane_f < HEAD_DIM

    def block(j, carry):
        rows = pl.ds(pl.multiple_of(j * QBLK, QBLK), QBLK)
        krows = pl.ds(pl.multiple_of(j * QBLK, QBLK), 2 * QBLK)
        table = jnp.where(j == 0, first, 0)
        lse_tile = jnp.zeros((QBLK, 2 * HEAD_DIM), F32)
        for hp in range(N_HEADS // 2):
            cs = slice(2 * HEAD_DIM * hp, 2 * HEAD_DIM * (hp + 1))
            q = q_ref[rows, cs]
            kk = k_scr[krows, cs]
            vv = v_scr[krows, cs]
            halves = []
            for e in range(2):
                h = 2 * hp + e
                s = lax.dot_general(q * head_mask[e], kk, (((1,), (1,)), ((), ())),
                                    preferred_element_type=F32)
                s = s * (HEAD_DIM ** -0.5) + bias_ref[table, h]
                m = jnp.max(s, axis=-1, keepdims=True)
                p = jnp.exp(s - m)
                denom = jnp.sum(p, axis=-1, keepdims=True)
                pv = jnp.dot(p.astype(BF16), vv, preferred_element_type=F32)
                halves.append(pv * (1.0 / denom))
                lse_tile = jnp.where(lane_f == h, m + jnp.log(denom), lse_tile)
            o_ref[rows, cs] = jnp.where(low_half, halves[0], halves[1]).astype(BF16)
        lse_ref[rows, :] = lse_tile
        return carry

    lax.fori_loop(0, ATTN_TILE // QBLK, block, 0)


def _attention_branch(q, k, v, seq_len, dilation):
    rows = q.shape[0]
    ratio = ATTN_TILE // QBLK
    cur = pl.BlockSpec((ATTN_TILE, ATTN_WIDTH), lambda i: (i, 0))
    prev = pl.BlockSpec((QBLK, ATTN_WIDTH), lambda i: (jnp.maximum(i * ratio - 1, 0), 0))
    return pl.pallas_call(
        functools.partial(_attn_kernel, tiles_per_seq=seq_len // ATTN_TILE, dilation=dilation),
        grid=(rows // ATTN_TILE,),
        in_specs=[cur, cur, prev, cur, prev],
        out_specs=[cur, pl.BlockSpec((ATTN_TILE, 2 * HEAD_DIM), lambda i: (i, 0))],
        out_shape=[jax.ShapeDtypeStruct((rows, ATTN_WIDTH), BF16),
                   jax.ShapeDtypeStruct((rows, 2 * HEAD_DIM), F32)],
        scratch_shapes=[pltpu.VMEM((ATTN_TILE + QBLK, ATTN_WIDTH), BF16)] * 2
        + [pltpu.VMEM((2, N_HEADS, QBLK, 2 * QBLK), F32)],
        compiler_params=pltpu.CompilerParams(dimension_semantics=("arbitrary",)),
        name=f"attn_s{seq_len}",
    )(q, k, k, v, v)


def _cexp(n, lam_re, lam_im):
    mag = jnp.exp(n * lam_re)
    return mag * jnp.cos(n * lam_im), mag * jnp.sin(n * lam_im)


def _ssm_kernel(u_ref, colp_ref, rowp_ref, dcol_ref, y_ref, a1_scr, cw_scr, *, chunks_per_seq):
    T, P, C = SSM_CHUNK, STATE_DIM, SSM_GROUP
    n_rows = C * T
    n_cols = u_ref.shape[-1]
    lane = lax.broadcasted_iota(jnp.int32, (1, LANES), 1)
    low = lane < P

    a_re = colp_ref[0, :, 0:1]
    a_im = colp_ref[0, :, 1:2]
    dt = jnp.exp(colp_ref[0, :, 34:35])
    lam_re, lam_im = dt * a_re, dt * a_im
    ab_re, ab_im = _cexp(1.0, lam_re, lam_im)
    inv_a2 = 1.0 / (a_re * a_re + a_im * a_im)
    nr, ni = ab_re - 1.0, ab_im
    cf_re = (nr * a_re + ni * a_im) * inv_a2
    cf_im = (ni * a_re - nr * a_im) * inv_a2

    rev = (T - 1 - (lane & (T - 1))).astype(F32)
    pw_re, pw_im = _cexp(rev, lam_re, lam_im)
    g_re, g_im = [], []
    for k in range(C // 2):
        b_re = jnp.where(low, colp_ref[0, :, 2 + 2 * k:3 + 2 * k], colp_ref[0, :, 3 + 2 * k:4 + 2 * k])
        b_im = jnp.where(low, colp_ref[0, :, 18 + 2 * k:19 + 2 * k], colp_ref[0, :, 19 + 2 * k:20 + 2 * k])
        bb_re = cf_re * b_re - cf_im * b_im
        bb_im = cf_re * b_im + cf_im * b_re
        g_re.append(pw_re * bb_re - pw_im * bb_im)
        g_im.append(pw_re * bb_im + pw_im * bb_re)
    gm = jnp.concatenate([jnp.concatenate(g_re, axis=1), jnp.concatenate(g_im, axis=1)], axis=0)
    a1_scr[n_rows:, :] = gm.astype(BF16)

    c_re2 = rowp_ref[0, 0:C, :]
    c_im2 = rowp_ref[0, C:2 * C, :]
    kr = jnp.dot(jnp.where(low, c_re2, -c_im2), gm, precision=lax.Precision.HIGHEST,
                 preferred_element_type=F32)
    krow = lax.broadcasted_iota(jnp.int32, (C, C * T), 0)
    klane = lax.broadcasted_iota(jnp.int32, (C, C * T), 1)
    kr = kr + jnp.where(klane == krow * T + (T - 1), dcol_ref[0, :, 0:1], 0.0)

    t_idx = lax.broadcasted_iota(jnp.int32, (T, LANES), 0)
    s_idx = lax.broadcasted_iota(jnp.int32, (T, LANES), 1) & (T - 1)
    causal = s_idx <= t_idx
    for c in range(C):
        for k in range(C * T // LANES):
            z = jnp.broadcast_to(kr[c:c + 1, k * LANES:(k + 1) * LANES], (T, LANES))
            z = pltpu.roll(z, LANES - (T - 1), axis=1, stride=1, stride_axis=0)
            a1_scr[c * T:(c + 1) * T, k * LANES:(k + 1) * LANES] = jnp.where(causal, z, 0.0).astype(BF16)

    dt_r = jnp.exp(rowp_ref[0, 34:35, :])
    steps = (lax.broadcasted_iota(jnp.int32, (T, LANES), 0) + 1).astype(F32)
    pr, pi = _cexp(steps, dt_r * rowp_ref[0, 32:33, :], dt_r * rowp_ref[0, 33:34, :])
    for c in range(C):
        cr, ci = c_re2[c:c + 1, :], c_im2[c:c + 1, :]
        cw_scr[c * T:(c + 1) * T, :] = jnp.where(low, cr * pr - ci * pi, -(cr * pi + ci * pr)).astype(BF16)

    u = u_ref[...].reshape(n_rows, n_cols)
    y1 = jnp.dot(a1_scr[...], u, preferred_element_type=F32)
    xr = y1[n_rows:n_rows + P]
    xi = y1[n_rows + P:]
    kidx = lax.broadcasted_iota(jnp.int32, (P, n_cols), 1) & (chunks_per_seq - 1)
    mr, mi = _cexp(float(T), lam_re, lam_im)
    sh = 1
    while sh < chunks_per_seq:
        rr = pltpu.roll(xr, sh, axis=1)
        ri = pltpu.roll(xi, sh, axis=1)
        ok = kidx >= sh
        xr, xi = (xr + jnp.where(ok, mr * rr - mi * ri, 0.0),
                  xi + jnp.where(ok, mr * ri + mi * rr, 0.0))
        mr, mi = mr * mr - mi * mi, 2.0 * mr * mi
        sh *= 2
    ok = kidx >= 1
    hr = jnp.where(ok, pltpu.roll(xr, 1, axis=1), 0.0)
    hi = jnp.where(ok, pltpu.roll(xi, 1, axis=1), 0.0)
    h = jnp.concatenate([hr, hi], axis=0).astype(BF16)
    y = y1[:n_rows] + jnp.dot(cw_scr[...], h, preferred_element_type=F32)
    y_ref[...] = jax.nn.gelu(y).astype(BF16).reshape(C, T, n_cols)


def _ssm(ut, colp, rowp, dcol, chunks_per_seq):
    n_cols = ut.shape[-1]
    n_rows = SSM_GROUP * SSM_CHUNK
    blk = pl.BlockSpec((SSM_GROUP, SSM_CHUNK, n_cols), lambda g: (g, 0, 0))
    per_group = lambda a: pl.BlockSpec((1,) + a.shape[1:], lambda g: (g, 0, 0))
    return pl.pallas_call(
        functools.partial(_ssm_kernel, chunks_per_seq=chunks_per_seq),
        grid=(N_SSM_GROUPS,),
        in_specs=[blk, per_group(colp), per_group(rowp), per_group(dcol)],
        out_specs=blk,
        out_shape=jax.ShapeDtypeStruct(ut.shape, BF16),
        scratch_shapes=[pltpu.VMEM((n_rows + 2 * STATE_DIM, n_rows), BF16),
                        pltpu.VMEM((n_rows, 2 * STATE_DIM), BF16)],
        compiler_params=pltpu.CompilerParams(dimension_semantics=("arbitrary",)),
        name="ssm",
    )(ut, colp, rowp, dcol)


def _pack_ssm_params(a_re, a_im, log_dt, b_re, b_im, c_re, c_im, d_skip):
    G, P, C = N_SSM_GROUPS, STATE_DIM, SSM_GROUP
    f = lambda t: t.astype(F32)
    log_dt_col = jnp.broadcast_to(f(log_dt)[:, None, None], (G, P, 1))
    colp = jnp.concatenate([f(a_re)[:, :, None], f(a_im)[:, :, None], f(b_re), f(b_im), log_dt_col,
                            jnp.zeros((G, P, LANES - 3 - 2 * C), F32)], axis=2)
    twice = lambda t: jnp.concatenate([f(t), f(t)], axis=-1)
    rowp = jnp.concatenate([twice(c_re), twice(c_im), twice(a_re)[:, None, :], twice(a_im)[:, None, :],
                            jnp.broadcast_to(f(log_dt)[:, None, None], (G, 1, 2 * P)),
                            jnp.zeros((G, 5, 2 * P), F32)], axis=1)
    dcol = jnp.broadcast_to(f(d_skip).reshape(G, C, 1), (G, C, LANES))
    return colp, rowp, dcol


def _mix_kernel(x1_ref, o1_ref, o4_ref, o16_ref, l1_ref, l4_ref, l16_ref, z_ref,
                wglu_ref, bglu_ref, wout_ref, gpost_ref, x2_ref, o_scr, l_scr):
    tm = TOKEN_TILE
    for n, (d, o_ref, l_ref) in enumerate(((4, o4_ref, l4_ref), (16, o16_ref, l16_ref))):
        for r in range(d):
            for c in range(ATTN_WIDTH // LANES):
                o_scr[n, c, pl.ds(r, tm // d, stride=d), :] = o_ref[0, r, :, c * LANES:(c + 1) * LANES].astype(F32)
            l_scr[n, pl.ds(r, tm // d, stride=d), :] = l_ref[0, r]
    lses = (l1_ref[0], l_scr[0], l_scr[1])
    m = jnp.maximum(jnp.maximum(lses[0], lses[1]), lses[2])
    es = [jnp.exp(l - m) for l in lses]
    inv = 1.0 / (es[0] + es[1] + es[2])
    ws = [e * inv for e in es]
    low_half = lax.broadcasted_iota(jnp.int32, (tm, 2 * HEAD_DIM), 1) < HEAD_DIM
    pairs = []
    for hp in range(N_HEADS // 2):
        cs = slice(2 * HEAD_DIM * hp, 2 * HEAD_DIM * (hp + 1))
        os_ = (o1_ref[0, :, cs].astype(F32), o_scr[0, hp], o_scr[1, hp])
        acc = jnp.zeros((tm, 2 * HEAD_DIM), F32)
        for w, o in zip(ws, os_):
            wexp = jnp.where(low_half, w[:, 2 * hp:2 * hp + 1], w[:, 2 * hp + 1:2 * hp + 2])
            acc = acc + wexp * o
        pairs.append(acc)
    attn = jnp.concatenate(pairs, axis=-1).astype(BF16)
    z = z_ref[0]
    gate = jax.nn.sigmoid(jnp.dot(z, wglu_ref[...], preferred_element_type=F32) + bglu_ref[...])
    ssm = (z.astype(F32) * gate).astype(BF16)
    mixed = (jnp.dot(attn, wout_ref[0:ATTN_WIDTH, :], preferred_element_type=F32)
             + jnp.dot(ssm, wout_ref[ATTN_WIDTH:, :], preferred_element_type=F32))
    x2_ref[0] = x1_ref[0] + _rms(mixed, gpost_ref[...])


def _mix(x1, o1, o4, o16, l1, l4, l16, z, wglu, bglu, wout, gpost):
    B, S, _ = x1.shape
    tm = TOKEN_TILE
    tok = lambda w: pl.BlockSpec((1, tm, w), lambda b, i: (b, i, 0))
    res = lambda d, w: pl.BlockSpec((1, d, tm // d, w), lambda b, i: (b, 0, i, 0))
    return pl.pallas_call(
        _mix_kernel,
        grid=(B, S // tm),
        in_specs=[tok(D_MODEL), tok(ATTN_WIDTH), res(4, ATTN_WIDTH), res(16, ATTN_WIDTH),
                  tok(2 * HEAD_DIM), res(4, 2 * HEAD_DIM), res(16, 2 * HEAD_DIM), tok(SSM_WIDTH),
                  _const_spec((SSM_WIDTH, SSM_WIDTH)), _const_spec((1, SSM_WIDTH)),
                  _const_spec((D_MODEL, D_MODEL)), _const_spec((1, D_MODEL))],
        out_specs=tok(D_MODEL),
        out_shape=jax.ShapeDtypeStruct((B, S, D_MODEL), F32),
        scratch_shapes=[pltpu.VMEM((2, ATTN_WIDTH // LANES, tm, LANES), F32),
                        pltpu.VMEM((2, tm, 2 * HEAD_DIM), F32)],
        compiler_params=pltpu.CompilerParams(
            dimension_semantics=("arbitrary", "arbitrary"), vmem_limit_bytes=VMEM_LIMIT_BYTES),
        name="mix",
    )(x1, o1, o4, o16, l1, l4, l16, z, wglu, bglu, wout, gpost)


def _ffn_kernel(x_ref, gpre_ref, wg_ref, wu_ref, wo_ref, gpost_ref, y_ref):
    y_ref[0] = _swiglu_ffn(x_ref[0], gpre_ref[...], wg_ref, wu_ref, wo_ref, gpost_ref[...])


def _ffn(x, gpre, wg, wu, wo, gpost):
    B, S, _ = x.shape
    tok = pl.BlockSpec((1, TOKEN_TILE, D_MODEL), lambda b, i: (b, i, 0))
    return pl.pallas_call(
        _ffn_kernel,
        grid=(B, S // TOKEN_TILE),
        in_specs=[tok, _const_spec((1, D_MODEL)),
                  _const_spec((D_MODEL, D_FF)), _const_spec((D_MODEL, D_FF)), _const_spec((D_FF, D_MODEL)),
                  _const_spec((1, D_MODEL))],
        out_specs=tok,
        out_shape=jax.ShapeDtypeStruct(x.shape, F32),
        compiler_params=pltpu.CompilerParams(
            dimension_semantics=("arbitrary", "arbitrary"), vmem_limit_bytes=VMEM_LIMIT_BYTES),
        name="ffn2",
    )(x, gpre, wg, wu, wo, gpost)


def _split_ffn_weights(w_in, w_out):
    return w_in[:, :D_FF].astype(BF16), w_in[:, D_FF:].astype(BF16), w_out.astype(BF16)


def _row(v):
    return v.astype(F32).reshape(1, -1)


def kernel(x, ffn1_pre_g, ffn1_w_in, ffn1_w_out, ffn1_post_g, mix_pre_g, w_mix_in, a_re, a_im, log_dt, b_re, b_im, c_re, c_im, d_skip, w_glu, b_glu, w_mix_out, mix_post_g, ffn2_pre_g, ffn2_w_in, ffn2_w_out, ffn2_post_g):
    B, S, _ = x.shape
    depth = ffn1_pre_g.shape[0]
    n_chunks = S // SSM_CHUNK
    for l in range(depth):
        wg1, wu1, wo1 = _split_ffn_weights(ffn1_w_in[l], ffn1_w_out[l])
        wg2, wu2, wo2 = _split_ffn_weights(ffn2_w_in[l], ffn2_w_out[l])
        (x1, q1, k1, v1, q4, k4, v4, q16, k16, v16, u) = _ffn1_proj(
            x, _row(ffn1_pre_g[l]), wg1, wu1, wo1, _row(ffn1_post_g[l]), _row(mix_pre_g[l]),
            w_mix_in[l].astype(BF16))

        outs = []
        for d, (q, k, v) in zip(DILATIONS, ((q1, k1, v1), (q4, k4, v4), (q16, k16, v16))):
            flat = lambda t: t.reshape(B * S, ATTN_WIDTH)
            o, lse = _attention_branch(flat(q), flat(k), flat(v), S // d, d)
            outs.append((o, lse))
        o1 = outs[0][0].reshape(B, S, ATTN_WIDTH)
        l1 = outs[0][1].reshape(B, S, 2 * HEAD_DIM)
        o4 = outs[1][0].reshape(B, 4, S // 4, ATTN_WIDTH)
        l4 = outs[1][1].reshape(B, 4, S // 4, 2 * HEAD_DIM)
        o16 = outs[2][0].reshape(B, 16, S // 16, ATTN_WIDTH)
        l16 = outs[2][1].reshape(B, 16, S // 16, 2 * HEAD_DIM)

        colp, rowp, dcol = _pack_ssm_params(a_re[l], a_im[l], log_dt[l], b_re[l], b_im[l], c_re[l], c_im[l],
                                            d_skip[l])
        ut = u.reshape(B, n_chunks, SSM_CHUNK, SSM_WIDTH).transpose(3, 2, 0, 1)
        ut = ut.reshape(SSM_WIDTH, SSM_CHUNK, B * n_chunks)
        zt = _ssm(ut, colp, rowp, dcol, n_chunks)
        z = zt.reshape(SSM_WIDTH, SSM_CHUNK, B, n_chunks).transpose(2, 3, 1, 0).reshape(B, S, SSM_WIDTH)

        x2 = _mix(x1, o1, o4, o16, l1, l4, l16, z, w_glu[l].astype(BF16), _row(b_glu[l]),
                  w_mix_out[l].astype(BF16), _row(mix_post_g[l]))
        x = _ffn(x2, _row(ffn2_pre_g[l]), wg2, wu2, wo2, _row(ffn2_post_g[l]))
    return x
```

```python
import functools

import jax
import jax.numpy as jnp
from jax import lax
from jax.experimental import pallas as pl
from jax.experimental.pallas import tpu as pltpu

F32 = jnp.float32
BF16 = jnp.bfloat16

D_MODEL = 1024
ATTN_WIDTH = 512
SSM_WIDTH = 512
HEAD_DIM = 64
N_HEADS = 8
DILATIONS = (1, 4, 16)
WINDOW_STEPS = 128
QBLK = 128
SSM_GROUP = 16
N_SSM_GROUPS = 32
STATE_DIM = 64
D_FF = 2816
NORM_EPS = 1e-6

TOKEN_TILE = 512
FF_CHUNK = 256
SSM_CHUNK = 64
ATTN_TILE = 512
MASK_VALUE = -1e30
LANES = 128
LOG2_E = 1.4426950408889634
QK_SCALE_LOG2 = HEAD_DIM ** -0.5 * LOG2_E
STATS_WIDTH = 4 * HEAD_DIM
VMEM_LIMIT_BYTES = 56 * 1024 * 1024


def _rms(x, g):
    return x * lax.rsqrt(jnp.mean(x * x, axis=-1, keepdims=True) + NORM_EPS) * g


def _swiglu_ffn(x, pre_g, wg_ref, wu_ref, wo_ref, post_g):
    h = _rms(x, pre_g).astype(BF16)
    acc = jnp.zeros((x.shape[0], D_MODEL), F32)
    for j in range(D_FF // FF_CHUNK):
        sl = slice(j * FF_CHUNK, (j + 1) * FF_CHUNK)
        gate = jnp.dot(h, wg_ref[:, sl], preferred_element_type=F32)
        up = jnp.dot(h, wu_ref[:, sl], preferred_element_type=F32)
        act = (gate * jax.nn.sigmoid(gate) * up).astype(BF16)
        acc = acc + jnp.dot(act, wo_ref[sl, :], preferred_element_type=F32)
    return x + 0.5 * _rms(acc, post_g)


def _ffn1_proj_kernel(x_ref, g1_ref, wg_ref, wu_ref, wo_ref, p1_ref, gm_ref, wm_ref,
                      x1_ref, q1_ref, k1_ref, v1_ref, q4_ref, k4_ref, v4_ref,
                      q16_ref, k16_ref, v16_ref, u_ref, proj_scr):
    x1 = _swiglu_ffn(x_ref[0], g1_ref[...], wg_ref, wu_ref, wo_ref, p1_ref[...])
    x1_ref[0] = x1
    h = _rms(x1, gm_ref[...]).astype(BF16)
    proj = jnp.dot(h, wm_ref[...], preferred_element_type=F32)
    qkv = (proj[:, :ATTN_WIDTH] * QK_SCALE_LOG2, proj[:, ATTN_WIDTH:2 * ATTN_WIDTH],
           proj[:, 2 * ATTN_WIDTH:3 * ATTN_WIDTH])
    lanes_per = ATTN_WIDTH // LANES
    for cb in range(3 * lanes_per):
        t, c = divmod(cb, lanes_per)
        proj_scr[cb] = qkv[t][:, c * LANES:(c + 1) * LANES]
    outs = ((q1_ref, q4_ref, q16_ref), (k1_ref, k4_ref, k16_ref), (v1_ref, v4_ref, v16_ref))
    for t, (o1, o4, o16) in enumerate(outs):
        o1[0] = qkv[t].astype(BF16)
        for d, o in ((4, o4), (16, o16)):
            for r in range(d):
                for c in range(lanes_per):
                    o[0, r, :, c * LANES:(c + 1) * LANES] = proj_scr[
                        t * lanes_per + c, pl.ds(r, TOKEN_TILE // d, stride=d), :].astype(BF16)
    u_ref[0] = proj[:, 3 * ATTN_WIDTH:].astype(BF16)


def _const_spec(shape):
    return pl.BlockSpec(shape, lambda *_: (0,) * len(shape), pipeline_mode=pl.Buffered(1))


def _ffn1_proj(x, g1, wg, wu, wo, p1, gm, wm):
    B, S, _ = x.shape
    tm = TOKEN_TILE
    tok = lambda w: pl.BlockSpec((1, tm, w), lambda b, i: (b, i, 0))
    res = lambda d: pl.BlockSpec((1, d, tm // d, ATTN_WIDTH), lambda b, i: (b, 0, i, 0))
    nat = jax.ShapeDtypeStruct((B, S, ATTN_WIDTH), BF16)
    r4 = jax.ShapeDtypeStruct((B, 4, S // 4, ATTN_WIDTH), BF16)
    r16 = jax.ShapeDtypeStruct((B, 16, S // 16, ATTN_WIDTH), BF16)
    return pl.pallas_call(
        _ffn1_proj_kernel,
        grid=(B, S // tm),
        in_specs=[tok(D_MODEL), _const_spec((1, D_MODEL)),
                  _const_spec((D_MODEL, D_FF)), _const_spec((D_MODEL, D_FF)), _const_spec((D_FF, D_MODEL)),
                  _const_spec((1, D_MODEL)), _const_spec((1, D_MODEL)), _const_spec((D_MODEL, 4 * ATTN_WIDTH))],
        out_specs=[tok(D_MODEL)] + [tok(ATTN_WIDTH)] * 3 + [res(4)] * 3 + [res(16)] * 3 + [tok(SSM_WIDTH)],
        out_shape=[jax.ShapeDtypeStruct((B, S, D_MODEL), F32)] + [nat] * 3 + [r4] * 3 + [r16] * 3 + [nat],
        scratch_shapes=[pltpu.VMEM((3 * ATTN_WIDTH // LANES, tm, LANES), F32)],
        compiler_params=pltpu.CompilerParams(
            dimension_semantics=("arbitrary", "arbitrary"), vmem_limit_bytes=VMEM_LIMIT_BYTES),
        name="ffn1_proj",
    )(x, g1, wg, wu, wo, p1, gm, wm)


def _attn_kernel(q_ref, kc_ref, kp_ref, vc_ref, vp_ref, o_ref, st_ref, k_scr, v_scr, bias_ref,
                 *, tiles_per_seq, dilation):
    @pl.when(pl.program_id(0) == 0)
    def _build_bias_tables():
        qi = lax.broadcasted_iota(jnp.int32, (QBLK, 2 * QBLK), 0)
        ci = lax.broadcasted_iota(jnp.int32, (QBLK, 2 * QBLK), 1)
        steps = QBLK + qi - ci
        dist = (steps * dilation).astype(F32)
        band = jnp.where(steps >= 0, jnp.where(steps <= WINDOW_STEPS, 1, 0), 0)
        band_first = jnp.where(ci >= QBLK, band, 0)
        for h in range(N_HEADS):
            bias = -(2.0 ** (-8.0 * (h + 1) / N_HEADS) * LOG2_E) * dist
            bias_ref[0, h] = jnp.where(band == 1, bias, MASK_VALUE)
            bias_ref[1, h] = jnp.where(band_first == 1, bias, MASK_VALUE)

    first = (lax.rem(pl.program_id(0), tiles_per_seq) == 0).astype(jnp.int32)
    k_scr[0:QBLK] = kp_ref[...]
    k_scr[QBLK:] = kc_ref[...]
    v_scr[0:QBLK] = vp_ref[...]
    v_scr[QBLK:] = vc_ref[...]
    lane = lax.broadcasted_iota(jnp.int32, (1, 2 * HEAD_DIM), 1)
    head_mask = ((lane < HEAD_DIM).astype(BF16), (lane >= HEAD_DIM).astype(BF16))
    ones_cols = tuple(jnp.broadcast_to(mk, (2 * QBLK, 2 * HEAD_DIM)) for mk in head_mask)
    lane_f = lax.broadcasted_iota(jnp.int32, (QBLK, 2 * HEAD_DIM), 1)
    low_half = lane_f < HEAD_DIM

    def block(j, carry):
        rows = pl.ds(pl.multiple_of(j * QBLK, QBLK), QBLK)
        krows = pl.ds(pl.multiple_of(j * QBLK, QBLK), 2 * QBLK)
        table = jnp.where(j == 0, first, 0)
        m_tile = jnp.zeros((QBLK, 2 * HEAD_DIM), F32)
        l_tile = jnp.ones((QBLK, 2 * HEAD_DIM), F32)
        for hp in range(N_HEADS // 2):
            cs = slice(2 * HEAD_DIM * hp, 2 * HEAD_DIM * (hp + 1))
            q = q_ref[rows, cs]
            kk = k_scr[krows, cs]
            vv = v_scr[krows, cs]
            ps, ms, vas = [], [], []
            for e in range(2):
                s = lax.dot_general(q * head_mask[e], kk, (((1,), (1,)), ((), ())),
                                    preferred_element_type=F32)
                s = s + bias_ref[table, 2 * hp + e]
                m = jnp.max(s, axis=-1, keepdims=True)
                ps.append(jnp.exp2(s - m).astype(BF16))
                ms.append(m)
                vas.append(jnp.concatenate([vv * head_mask[e], ones_cols[e]], axis=1))
            pv = jnp.dot(jnp.concatenate(ps, axis=1), jnp.concatenate(vas, axis=0),
                         preferred_element_type=F32)
            o_ref[rows, cs] = pv[:, :2 * HEAD_DIM].astype(BF16)
            mine = (lane_f & (HEAD_DIM - 1)) == hp
            m_tile = jnp.where(mine, jnp.where(low_half, ms[0], ms[1]), m_tile)
            l_tile = jnp.where(mine, pv[:, 2 * HEAD_DIM:], l_tile)
        st_ref[rows, 0:2 * HEAD_DIM] = m_tile
        st_ref[rows, 2 * HEAD_DIM:] = l_tile
        return carry

    lax.fori_loop(0, ATTN_TILE // QBLK, block, 0)


def _attention_branch(q, k, v, seq_len, dilation):
    rows = q.shape[0]
    ratio = ATTN_TILE // QBLK
    cur = pl.BlockSpec((ATTN_TILE, ATTN_WIDTH), lambda i: (i, 0))
    prev = pl.BlockSpec((QBLK, ATTN_WIDTH), lambda i: (jnp.maximum(i * ratio - 1, 0), 0))
    return pl.pallas_call(
        functools.partial(_attn_kernel, tiles_per_seq=seq_len // ATTN_TILE, dilation=dilation),
        grid=(rows // ATTN_TILE,),
        in_specs=[cur, cur, prev, cur, prev],
        out_specs=[cur, pl.BlockSpec((ATTN_TILE, STATS_WIDTH), lambda i: (i, 0))],
        out_shape=[jax.ShapeDtypeStruct((rows, ATTN_WIDTH), BF16),
                   jax.ShapeDtypeStruct((rows, STATS_WIDTH), F32)],
        scratch_shapes=[pltpu.VMEM((ATTN_TILE + QBLK, ATTN_WIDTH), BF16)] * 2
        + [pltpu.VMEM((2, N_HEADS, QBLK, 2 * QBLK), F32)],
        compiler_params=pltpu.CompilerParams(dimension_semantics=("arbitrary",)),
        name=f"attn_s{seq_len}",
    )(q, k, k, v, v)


def _cexp(n, lam_re, lam_im):
    mag = jnp.exp(n * lam_re)
    return mag * jnp.cos(n * lam_im), mag * jnp.sin(n * lam_im)


def _ssm_kernel(u_ref, colp_ref, rowp_ref, dcol_ref, y_ref, a1_scr, cw_scr, *, chunks_per_seq):
    T, P, C = SSM_CHUNK, STATE_DIM, SSM_GROUP
    n_rows = C * T
    n_cols = u_ref.shape[-1]
    lane = lax.broadcasted_iota(jnp.int32, (1, LANES), 1)
    low = lane < P

    a_re = colp_ref[0, :, 0:1]
    a_im = colp_ref[0, :, 1:2]
    dt = jnp.exp(colp_ref[0, :, 34:35])
    lam_re, lam_im = dt * a_re, dt * a_im
    ab_re, ab_im = _cexp(1.0, lam_re, lam_im)
    inv_a2 = 1.0 / (a_re * a_re + a_im * a_im)
    nr, ni = ab_re - 1.0, ab_im
    cf_re = (nr * a_re + ni * a_im) * inv_a2
    cf_im = (ni * a_re - nr * a_im) * inv_a2

    rev = (T - 1 - (lane & (T - 1))).astype(F32)
    pw_re, pw_im = _cexp(rev, lam_re, lam_im)
    g_re, g_im = [], []
    for k in range(C // 2):
        b_re = jnp.where(low, colp_ref[0, :, 2 + 2 * k:3 + 2 * k], colp_ref[0, :, 3 + 2 * k:4 + 2 * k])
        b_im = jnp.where(low, colp_ref[0, :, 18 + 2 * k:19 + 2 * k], colp_ref[0, :, 19 + 2 * k:20 + 2 * k])
        bb_re = cf_re * b_re - cf_im * b_im
        bb_im = cf_re * b_im + cf_im * b_re
        g_re.append(pw_re * bb_re - pw_im * bb_im)
        g_im.append(pw_re * bb_im + pw_im * bb_re)
    gm = jnp.concatenate([jnp.concatenate(g_re, axis=1), jnp.concatenate(g_im, axis=1)], axis=0)
    a1_scr[n_rows:, :] = gm.astype(BF16)

    c_re2 = rowp_ref[0, 0:C, :]
    c_im2 = rowp_ref[0, C:2 * C, :]
    kr = jnp.dot(jnp.where(low, c_re2, -c_im2), gm, precision=lax.Precision.HIGHEST,
                 preferred_element_type=F32)
    krow = lax.broadcasted_iota(jnp.int32, (C, C * T), 0)
    klane = lax.broadcasted_iota(jnp.int32, (C, C * T), 1)
    kr = kr + jnp.where(klane == krow * T + (T - 1), dcol_ref[0, :, 0:1], 0.0)

    t_idx = lax.broadcasted_iota(jnp.int32, (T, LANES), 0)
    s_idx = lax.broadcasted_iota(jnp.int32, (T, LANES), 1) & (T - 1)
    causal = s_idx <= t_idx
    for c in range(C):
        for k in range(C * T // LANES):
            z = jnp.broadcast_to(kr[c:c + 1, k * LANES:(k + 1) * LANES], (T, LANES))
            z = pltpu.roll(z, LANES - (T - 1), axis=1, stride=1, stride_axis=0)
            a1_scr[c * T:(c + 1) * T, k * LANES:(k + 1) * LANES] = jnp.where(causal, z, 0.0).astype(BF16)

    dt_r = jnp.exp(rowp_ref[0, 34:35, :])
    steps = (lax.broadcasted_iota(jnp.int32, (T, LANES), 0) + 1).astype(F32)
    pr, pi = _cexp(steps, dt_r * rowp_ref[0, 32:33, :], dt_r * rowp_ref[0, 33:34, :])
    for c in range(C):
        cr, ci = c_re2[c:c + 1, :], c_im2[c:c + 1, :]
        cw_scr[c * T:(c + 1) * T, :] = jnp.where(low, cr * pr - ci * pi, -(cr * pi + ci * pr)).astype(BF16)

    u = u_ref[...].reshape(n_rows, n_cols)
    y1 = jnp.dot(a1_scr[...], u, preferred_element_type=F32)
    xr = y1[n_rows:n_rows + P]
    xi = y1[n_rows + P:]
    kidx = lax.broadcasted_iota(jnp.int32, (P, n_cols), 1) & (chunks_per_seq - 1)
    mr, mi = _cexp(float(T), lam_re, lam_im)
    sh = 1
    while sh < chunks_per_seq:
        rr = pltpu.roll(xr, sh, axis=1)
        ri = pltpu.roll(xi, sh, axis=1)
        ok = kidx >= sh
        xr, xi = (xr + jnp.where(ok, mr * rr - mi * ri, 0.0),
                  xi + jnp.where(ok, mr * ri + mi * rr, 0.0))
        mr, mi = mr * mr - mi * mi, 2.0 * mr * mi
        sh *= 2
    ok = kidx >= 1
    hr = jnp.where(ok, pltpu.roll(xr, 1, axis=1), 0.0)
    hi = jnp.where(ok, pltpu.roll(xi, 1, axis=1), 0.0)
    h = jnp.concatenate([hr, hi], axis=0).astype(BF16)
    y = y1[:n_rows] + jnp.dot(cw_scr[...], h, preferred_element_type=F32)
    y_ref[...] = jax.nn.gelu(y).astype(BF16).reshape(C, T, n_cols)


def _ssm(ut, colp, rowp, dcol, chunks_per_seq):
    n_cols = ut.shape[-1]
    n_rows = SSM_GROUP * SSM_CHUNK
    blk = pl.BlockSpec((SSM_GROUP, SSM_CHUNK, n_cols), lambda g: (g, 0, 0))
    per_group = lambda a: pl.BlockSpec((1,) + a.shape[1:], lambda g: (g, 0, 0))
    return pl.pallas_call(
        functools.partial(_ssm_kernel, chunks_per_seq=chunks_per_seq),
        grid=(N_SSM_GROUPS,),
        in_specs=[blk, per_group(colp), per_group(rowp), per_group(dcol)],
        out_specs=blk,
        out_shape=jax.ShapeDtypeStruct(ut.shape, BF16),
        scratch_shapes=[pltpu.VMEM((n_rows + 2 * STATE_DIM, n_rows), BF16),
                        pltpu.VMEM((n_rows, 2 * STATE_DIM), BF16)],
        compiler_params=pltpu.CompilerParams(dimension_semantics=("arbitrary",)),
        name="ssm",
    )(ut, colp, rowp, dcol)


def _pack_ssm_params(a_re, a_im, log_dt, b_re, b_im, c_re, c_im, d_skip):
    G, P, C = N_SSM_GROUPS, STATE_DIM, SSM_GROUP
    f = lambda t: t.astype(F32)
    log_dt_col = jnp.broadcast_to(f(log_dt)[:, None, None], (G, P, 1))
    colp = jnp.concatenate([f(a_re)[:, :, None], f(a_im)[:, :, None], f(b_re), f(b_im), log_dt_col,
                            jnp.zeros((G, P, LANES - 3 - 2 * C), F32)], axis=2)
    twice = lambda t: jnp.concatenate([f(t), f(t)], axis=-1)
    rowp = jnp.concatenate([twice(c_re), twice(c_im), twice(a_re)[:, None, :], twice(a_im)[:, None, :],
                            jnp.broadcast_to(f(log_dt)[:, None, None], (G, 1, 2 * P)),
                            jnp.zeros((G, 5, 2 * P), F32)], axis=1)
    dcol = jnp.broadcast_to(f(d_skip).reshape(G, C, 1), (G, C, LANES))
    return colp, rowp, dcol


def _mix_kernel(x1_ref, o1_ref, o4_ref, o16_ref, l1_ref, l4_ref, l16_ref, z_ref,
                wglu_ref, bglu_ref, wout_ref, gpost_ref, x2_ref, o_scr, l_scr):
    tm = TOKEN_TILE
    for n, (d, o_ref, l_ref) in enumerate(((4, o4_ref, l4_ref), (16, o16_ref, l16_ref))):
        for r in range(d):
            for c in range(ATTN_WIDTH // LANES):
                o_scr[n, c, pl.ds(r, tm // d, stride=d), :] = o_ref[0, r, :, c * LANES:(c + 1) * LANES].astype(F32)
            for c in range(STATS_WIDTH // LANES):
                l_scr[n, c, pl.ds(r, tm // d, stride=d), :] = l_ref[0, r, :, c * LANES:(c + 1) * LANES]
    ms = (l1_ref[0, :, 0:LANES], l_scr[0, 0], l_scr[1, 0])
    ls = (l1_ref[0, :, LANES:], l_scr[0, 1], l_scr[1, 1])
    m = jnp.maximum(jnp.maximum(ms[0], ms[1]), ms[2])
    es = [jnp.exp2(mn - m) for mn in ms]
    inv = 1.0 / (es[0] * ls[0] + es[1] * ls[1] + es[2] * ls[2])
    ws = [e * inv for e in es]
    low_half = lax.broadcasted_iota(jnp.int32, (tm, 2 * HEAD_DIM), 1) < HEAD_DIM
    pairs = []
    for hp in range(N_HEADS // 2):
        cs = slice(2 * HEAD_DIM * hp, 2 * HEAD_DIM * (hp + 1))
        os_ = (o1_ref[0, :, cs].astype(F32), o_scr[0, hp], o_scr[1, hp])
        acc = jnp.zeros((tm, 2 * HEAD_DIM), F32)
        for w, o in zip(ws, os_):
            wexp = jnp.where(low_half, w[:, hp:hp + 1], w[:, HEAD_DIM + hp:HEAD_DIM + hp + 1])
            acc = acc + wexp * o
        pairs.append(acc)
    attn = jnp.concatenate(pairs, axis=-1).astype(BF16)
    z = z_ref[0]
    gate = jax.nn.sigmoid(jnp.dot(z, wglu_ref[...], preferred_element_type=F32) + bglu_ref[...])
    ssm = (z.astype(F32) * gate).astype(BF16)
    mixed = (jnp.dot(attn, wout_ref[0:ATTN_WIDTH, :], preferred_element_type=F32)
             + jnp.dot(ssm, wout_ref[ATTN_WIDTH:, :], preferred_element_type=F32))
    x2_ref[0] = x1_ref[0] + _rms(mixed, gpost_ref[...])


def _mix(x1, o1, o4, o16, l1, l4, l16, z, wglu, bglu, wout, gpost):
    B, S, _ = x1.shape
    tm = TOKEN_TILE
    tok = lambda w: pl.BlockSpec((1, tm, w), lambda b, i: (b, i, 0))
    res = lambda d, w: pl.BlockSpec((1, d, tm // d, w), lambda b, i: (b, 0, i, 0))
    return pl.pallas_call(
        _mix_kernel,
        grid=(B, S // tm),
        in_specs=[tok(D_MODEL), tok(ATTN_WIDTH), res(4, ATTN_WIDTH), res(16, ATTN_WIDTH),
                  tok(STATS_WIDTH), res(4, STATS_WIDTH), res(16, STATS_WIDTH), tok(SSM_WIDTH),
                  _const_spec((SSM_WIDTH, SSM_WIDTH)), _const_spec((1, SSM_WIDTH)),
                  _const_spec((D_MODEL, D_MODEL)), _const_spec((1, D_MODEL))],
        out_specs=tok(D_MODEL),
        out_shape=jax.ShapeDtypeStruct((B, S, D_MODEL), F32),
        scratch_shapes=[pltpu.VMEM((2, ATTN_WIDTH // LANES, tm, LANES), F32),
                        pltpu.VMEM((2, STATS_WIDTH // LANES, tm, LANES), F32)],
        compiler_params=pltpu.CompilerParams(
            dimension_semantics=("arbitrary", "arbitrary"), vmem_limit_bytes=VMEM_LIMIT_BYTES),
        name="mix",
    )(x1, o1, o4, o16, l1, l4, l16, z, wglu, bglu, wout, gpost)


def _ffn_kernel(x_ref, gpre_ref, wg_ref, wu_ref, wo_ref, gpost_ref, y_ref):
    y_ref[0] = _swiglu_ffn(x_ref[0], gpre_ref[...], wg_ref, wu_ref, wo_ref, gpost_ref[...])


def _ffn(x, gpre, wg, wu, wo, gpost):
    B, S, _ = x.shape
    tok = pl.BlockSpec((1, TOKEN_TILE, D_MODEL), lambda b, i: (b, i, 0))
    return pl.pallas_call(
        _ffn_kernel,
        grid=(B, S // TOKEN_TILE),
        in_specs=[tok, _const_spec((1, D_MODEL)),
                  _const_spec((D_MODEL, D_FF)), _const_spec((D_MODEL, D_FF)), _const_spec((D_FF, D_MODEL)),
                  _const_spec((1, D_MODEL))],
        out_specs=tok,
        out_shape=jax.ShapeDtypeStruct(x.shape, F32),
        compiler_params=pltpu.CompilerParams(
            dimension_semantics=("arbitrary", "arbitrary"), vmem_limit_bytes=VMEM_LIMIT_BYTES),
        name="ffn2",
    )(x, gpre, wg, wu, wo, gpost)


def _split_ffn_weights(w_in, w_out):
    return w_in[:, :D_FF].astype(BF16), w_in[:, D_FF:].astype(BF16), w_out.astype(BF16)


def _row(v):
    return v.astype(F32).reshape(1, -1)


def kernel(x, ffn1_pre_g, ffn1_w_in, ffn1_w_out, ffn1_post_g, mix_pre_g, w_mix_in, a_re, a_im, log_dt, b_re, b_im, c_re, c_im, d_skip, w_glu, b_glu, w_mix_out, mix_post_g, ffn2_pre_g, ffn2_w_in, ffn2_w_out, ffn2_post_g):
    B, S, _ = x.shape
    depth = ffn1_pre_g.shape[0]
    n_chunks = S // SSM_CHUNK
    for l in range(depth):
        wg1, wu1, wo1 = _split_ffn_weights(ffn1_w_in[l], ffn1_w_out[l])
        wg2, wu2, wo2 = _split_ffn_weights(ffn2_w_in[l], ffn2_w_out[l])
        (x1, q1, k1, v1, q4, k4, v4, q16, k16, v16, u) = _ffn1_proj(
            x, _row(ffn1_pre_g[l]), wg1, wu1, wo1, _row(ffn1_post_g[l]), _row(mix_pre_g[l]),
            w_mix_in[l].astype(BF16))

        outs = []
        for d, (q, k, v) in zip(DILATIONS, ((q1, k1, v1), (q4, k4, v4), (q16, k16, v16))):
            flat = lambda t: t.reshape(B * S, ATTN_WIDTH)
            outs.append(_attention_branch(flat(q), flat(k), flat(v), S // d, d))
        o1 = outs[0][0].reshape(B, S, ATTN_WIDTH)
        l1 = outs[0][1].reshape(B, S, STATS_WIDTH)
        o4 = outs[1][0].reshape(B, 4, S // 4, ATTN_WIDTH)
        l4 = outs[1][1].reshape(B, 4, S // 4, STATS_WIDTH)
        o16 = outs[2][0].reshape(B, 16, S // 16, ATTN_WIDTH)
        l16 = outs[2][1].reshape(B, 16, S // 16, STATS_WIDTH)

        colp, rowp, dcol = _pack_ssm_params(a_re[l], a_im[l], log_dt[l], b_re[l], b_im[l], c_re[l], c_im[l],
                                            d_skip[l])
        ut = u.reshape(B, n_chunks, SSM_CHUNK, SSM_WIDTH).transpose(3, 2, 0, 1)
        ut = ut.reshape(SSM_WIDTH, SSM_CHUNK, B * n_chunks)
        zt = _ssm(ut, colp, rowp, dcol, n_chunks)
        z = zt.reshape(SSM_WIDTH, SSM_CHUNK, B, n_chunks).transpose(2, 3, 1, 0).reshape(B, S, SSM_WIDTH)

        x2 = _mix(x1, o1, o4, o16, l1, l4, l16, z, w_glu[l].astype(BF16), _row(b_glu[l]),
                  w_mix_out[l].astype(BF16), _row(mix_post_g[l]))
        x = _ffn(x2, _row(ffn2_pre_g[l]), wg2, wu2, wo2, _row(ffn2_post_g[l]))
    return x
```

```python
import functools

import jax
import jax.numpy as jnp
from jax import lax
from jax.experimental import pallas as pl
from jax.experimental.pallas import tpu as pltpu

F32 = jnp.float32
BF16 = jnp.bfloat16

D_MODEL = 1024
ATTN_WIDTH = 512
SSM_WIDTH = 512
HEAD_DIM = 64
N_HEADS = 8
DILATIONS = (1, 4, 16)
WINDOW_STEPS = 128
QBLK = 128
SSM_GROUP = 16
N_SSM_GROUPS = 32
STATE_DIM = 64
D_FF = 2816
NORM_EPS = 1e-6

TOKEN_TILE = 512
FF_CHUNK = 256
SSM_CHUNK = 64
ATTN_TILE = 1024
MASK_VALUE = -1e30
LANES = 128
LOG2_E = 1.4426950408889634
QK_SCALE_LOG2 = HEAD_DIM ** -0.5 * LOG2_E
STATS_WIDTH = 4 * HEAD_DIM
PROJ_SLABS = 4 * ATTN_WIDTH // LANES
VMEM_LIMIT_BYTES = 56 * 1024 * 1024


def _rms(x, g):
    return x * lax.rsqrt(jnp.mean(x * x, axis=-1, keepdims=True) + NORM_EPS) * g


def _swiglu_ffn(read_x, pre_g, wg_ref, wu_ref, wo_ref, post_g):
    h = _rms(read_x(), pre_g).astype(BF16)
    acc = jnp.zeros((h.shape[0], D_MODEL), F32)
    for j in range(D_FF // FF_CHUNK):
        sl = slice(j * FF_CHUNK, (j + 1) * FF_CHUNK)
        gate = jnp.dot(h, wg_ref[:, sl], preferred_element_type=F32)
        up = jnp.dot(h, wu_ref[:, sl], preferred_element_type=F32)
        act = (gate * jax.nn.sigmoid(gate) * up).astype(BF16)
        acc = acc + jnp.dot(act, wo_ref[sl, :], preferred_element_type=F32)
    return read_x() + 0.5 * _rms(acc, post_g)


def _const_spec(shape):
    return pl.BlockSpec(shape, lambda *_: (0,) * len(shape), pipeline_mode=pl.Buffered(1))


def _token_specs(n_tiles, tiles_per_seq, lag):
    tm = TOKEN_TILE

    def tile(i):
        return jnp.clip(i - lag, 0, n_tiles - 1)

    def tok(w):
        return pl.BlockSpec((tm, w), lambda i: (tile(i), 0))

    def res(d, w):
        return pl.BlockSpec((1, d, tm // d, w),
                            lambda i: (tile(i) // tiles_per_seq, 0, tile(i) % tiles_per_seq, 0))

    return tok, res


def _ffn1_proj_kernel(x_ref, g1_ref, wg_ref, wu_ref, wo_ref, p1_ref, gm_ref, wm_ref,
                      x1_ref, q1_ref, k1_ref, v1_ref, q4_ref, k4_ref, v4_ref,
                      q16_ref, k16_ref, v16_ref, u_ref, proj_scr, mod4_scr):
    @pl.when(pl.program_id(0) == 0)
    def _init():
        proj_scr[...] = jnp.zeros(proj_scr.shape, F32)

    lanes_per = ATTN_WIDTH // LANES
    q4_rows = TOKEN_TILE // 4
    outs = ((q1_ref, q4_ref, q16_ref), (k1_ref, k4_ref, k16_ref), (v1_ref, v4_ref, v16_ref))
    for t, (o1, o4, o16) in enumerate(outs):
        for c in range(lanes_per):
            cs = slice(c * LANES, (c + 1) * LANES)
            slab = t * lanes_per + c
            o1[:, cs] = proj_scr[slab].astype(BF16)
            for b in range(4):
                cls4 = proj_scr[slab, pl.ds(b, q4_rows, stride=4), :]
                o4[0, b, :, cs] = cls4.astype(BF16)
                mod4_scr[slab, b * q4_rows:(b + 1) * q4_rows, :] = cls4
            for b in range(4):
                for a in range(4):
                    o16[0, 4 * a + b, :, cs] = mod4_scr[
                        slab, pl.ds(b * q4_rows + a, q4_rows // 4, stride=4), :].astype(BF16)
    for c in range(lanes_per):
        u_ref[:, c * LANES:(c + 1) * LANES] = proj_scr[3 * lanes_per + c].astype(BF16)

    x1 = _swiglu_ffn(lambda: x_ref[...], g1_ref[...], wg_ref, wu_ref, wo_ref, p1_ref[...])
    x1_ref[...] = x1
    h = _rms(x1, gm_ref[...]).astype(BF16)
    proj = jnp.dot(h, wm_ref[...], preferred_element_type=F32)
    for cb in range(PROJ_SLABS):
        slab = proj[:, cb * LANES:(cb + 1) * LANES]
        proj_scr[cb] = slab * QK_SCALE_LOG2 if cb < lanes_per else slab


def _ffn1_proj(x, g1, wg, wu, wo, p1, gm, wm):
    B, S, _ = x.shape
    tm = TOKEN_TILE
    n_tiles = B * S // tm
    tok_now, _ = _token_specs(n_tiles, S // tm, 0)
    tok_prev, res_prev = _token_specs(n_tiles, S // tm, 1)
    nat = jax.ShapeDtypeStruct((B * S, ATTN_WIDTH), BF16)
    r4 = jax.ShapeDtypeStruct((B, 4, S // 4, ATTN_WIDTH), BF16)
    r16 = jax.ShapeDtypeStruct((B, 16, S // 16, ATTN_WIDTH), BF16)
    return pl.pallas_call(
        _ffn1_proj_kernel,
        grid=(n_tiles + 1,),
        in_specs=[tok_now(D_MODEL), _const_spec((1, D_MODEL)),
                  _const_spec((D_MODEL, D_FF)), _const_spec((D_MODEL, D_FF)), _const_spec((D_FF, D_MODEL)),
                  _const_spec((1, D_MODEL)), _const_spec((1, D_MODEL)), _const_spec((D_MODEL, 4 * ATTN_WIDTH))],
        out_specs=[tok_now(D_MODEL)] + [tok_prev(ATTN_WIDTH)] * 3 + [res_prev(4, ATTN_WIDTH)] * 3
        + [res_prev(16, ATTN_WIDTH)] * 3 + [tok_prev(SSM_WIDTH)],
        out_shape=[jax.ShapeDtypeStruct((B * S, D_MODEL), F32)] + [nat] * 3 + [r4] * 3 + [r16] * 3 + [nat],
        scratch_shapes=[pltpu.VMEM((PROJ_SLABS, tm, LANES), F32),
                        pltpu.VMEM((3 * ATTN_WIDTH // LANES, tm, LANES), F32)],
        compiler_params=pltpu.CompilerParams(
            dimension_semantics=("arbitrary",), vmem_limit_bytes=VMEM_LIMIT_BYTES),
        name="ffn1_proj",
    )(x.reshape(B * S, D_MODEL), g1, wg, wu, wo, p1, gm, wm)


def _attn_kernel(q_ref, kc_ref, kp_ref, vc_ref, vp_ref, o_ref, st_ref, bias_ref, *, blocks_per_seq, dilation):
    @pl.when(pl.program_id(0) == 0)
    def _build_bias_tables():
        qi = lax.broadcasted_iota(jnp.int32, (QBLK, 2 * QBLK), 0)
        ci = lax.broadcasted_iota(jnp.int32, (QBLK, 2 * QBLK), 1)
        steps = QBLK + qi - ci
        dist = (steps * dilation).astype(F32)
        band = jnp.where(steps >= 0, jnp.where(steps <= WINDOW_STEPS, 1, 0), 0)
        band_first = jnp.where(ci >= QBLK, band, 0)
        for h in range(N_HEADS):
            bias = -(2.0 ** (-8.0 * (h + 1) / N_HEADS) * LOG2_E) * dist
            bias_ref[0, h] = jnp.where(band == 1, bias, MASK_VALUE)
            bias_ref[1, h] = jnp.where(band_first == 1, bias, MASK_VALUE)

    n_blocks = ATTN_TILE // QBLK
    lane = lax.broadcasted_iota(jnp.int32, (1, 2 * HEAD_DIM), 1)
    head_mask = ((lane < HEAD_DIM).astype(BF16), (lane >= HEAD_DIM).astype(BF16))
    ones_cols = tuple(jnp.broadcast_to(mk, (2 * QBLK, 2 * HEAD_DIM)) for mk in head_mask)
    lane_f = lax.broadcasted_iota(jnp.int32, (QBLK, 2 * HEAD_DIM), 1)
    low_half = lane_f < HEAD_DIM

    for j in range(n_blocks):
        rows = slice(j * QBLK, (j + 1) * QBLK)
        if blocks_per_seq >= n_blocks:
            tiles_per_seq = blocks_per_seq // n_blocks
            table = (lax.rem(pl.program_id(0), tiles_per_seq) == 0).astype(jnp.int32) if j == 0 else 0
        else:
            table = 1 if j % blocks_per_seq == 0 else 0
        m_tile = jnp.zeros((QBLK, 2 * HEAD_DIM), F32)
        l_tile = jnp.ones((QBLK, 2 * HEAD_DIM), F32)
        for hp in range(N_HEADS // 2):
            cs = slice(2 * HEAD_DIM * hp, 2 * HEAD_DIM * (hp + 1))
            q = q_ref[rows, cs]
            if j == 0:
                kk = jnp.concatenate([kp_ref[:, cs], kc_ref[0:QBLK, cs]], axis=0)
                vv = jnp.concatenate([vp_ref[:, cs], vc_ref[0:QBLK, cs]], axis=0)
            else:
                kk = kc_ref[(j - 1) * QBLK:(j + 1) * QBLK, cs]
                vv = vc_ref[(j - 1) * QBLK:(j + 1) * QBLK, cs]
            ps, ms, vas = [], [], []
            for e in range(2):
                s = lax.dot_general(q * head_mask[e], kk, (((1,), (1,)), ((), ())),
                                    preferred_element_type=F32)
                s = s + bias_ref[table, 2 * hp + e]
                m = jnp.max(s, axis=-1, keepdims=True)
                ps.append(jnp.exp2(s - m).astype(BF16))
                ms.append(m)
                vas.append(jnp.concatenate([vv * head_mask[e], ones_cols[e]], axis=1))
            pv = jnp.dot(jnp.concatenate(ps, axis=1), jnp.concatenate(vas, axis=0),
                         preferred_element_type=F32)
            o_ref[rows, cs] = pv[:, :2 * HEAD_DIM].astype(BF16)
            mine = (lane_f & (HEAD_DIM - 1)) == hp
            m_tile = jnp.where(mine, jnp.where(low_half, ms[0], ms[1]), m_tile)
            l_tile = jnp.where(mine, pv[:, 2 * HEAD_DIM:], l_tile)
        st_ref[rows, 0:2 * HEAD_DIM] = m_tile
        st_ref[rows, 2 * HEAD_DIM:] = l_tile


def _attention_branch(q, k, v, seq_len, dilation):
    rows = q.shape[0]
    ratio = ATTN_TILE // QBLK
    cur = pl.BlockSpec((ATTN_TILE, ATTN_WIDTH), lambda i: (i, 0))
    prev = pl.BlockSpec((QBLK, ATTN_WIDTH), lambda i: (jnp.maximum(i * ratio - 1, 0), 0))
    return pl.pallas_call(
        functools.partial(_attn_kernel, blocks_per_seq=seq_len // QBLK, dilation=dilation),
        grid=(rows // ATTN_TILE,),
        in_specs=[cur, cur, prev, cur, prev],
        out_specs=[cur, pl.BlockSpec((ATTN_TILE, STATS_WIDTH), lambda i: (i, 0))],
        out_shape=[jax.ShapeDtypeStruct((rows, ATTN_WIDTH), BF16),
                   jax.ShapeDtypeStruct((rows, STATS_WIDTH), F32)],
        scratch_shapes=[pltpu.VMEM((2, N_HEADS, QBLK, 2 * QBLK), F32)],
        compiler_params=pltpu.CompilerParams(dimension_semantics=("arbitrary",)),
        name=f"attn_s{seq_len}",
    )(q, k, k, v, v)


def _cexp(n, lam_re, lam_im):
    mag = jnp.exp(n * lam_re)
    return mag * jnp.cos(n * lam_im), mag * jnp.sin(n * lam_im)


def _ssm_kernel(u_ref, colp_ref, rowp_ref, dcol_ref, y_ref, a1_scr, cw_scr, *, chunks_per_seq):
    T, P, C = SSM_CHUNK, STATE_DIM, SSM_GROUP
    n_rows = C * T
    n_cols = u_ref.shape[-1]
    lane = lax.broadcasted_iota(jnp.int32, (1, LANES), 1)
    low = lane < P

    a_re = colp_ref[0, :, 0:1]
    a_im = colp_ref[0, :, 1:2]
    dt = jnp.exp(colp_ref[0, :, 34:35])
    lam_re, lam_im = dt * a_re, dt * a_im
    ab_re, ab_im = _cexp(1.0, lam_re, lam_im)
    inv_a2 = 1.0 / (a_re * a_re + a_im * a_im)
    nr, ni = ab_re - 1.0, ab_im
    cf_re = (nr * a_re + ni * a_im) * inv_a2
    cf_im = (ni * a_re - nr * a_im) * inv_a2

    rev = (T - 1 - (lane & (T - 1))).astype(F32)
    pw_re, pw_im = _cexp(rev, lam_re, lam_im)
    g_re, g_im = [], []
    for k in range(C // 2):
        b_re = jnp.where(low, colp_ref[0, :, 2 + 2 * k:3 + 2 * k], colp_ref[0, :, 3 + 2 * k:4 + 2 * k])
        b_im = jnp.where(low, colp_ref[0, :, 18 + 2 * k:19 + 2 * k], colp_ref[0, :, 19 + 2 * k:20 + 2 * k])
        bb_re = cf_re * b_re - cf_im * b_im
        bb_im = cf_re * b_im + cf_im * b_re
        g_re.append(pw_re * bb_re - pw_im * bb_im)
        g_im.append(pw_re * bb_im + pw_im * bb_re)
    gm = jnp.concatenate([jnp.concatenate(g_re, axis=1), jnp.concatenate(g_im, axis=1)], axis=0)
    a1_scr[n_rows:, :] = gm.astype(BF16)

    c_re2 = rowp_ref[0, 0:C, :]
    c_im2 = rowp_ref[0, C:2 * C, :]
    kr = jnp.dot(jnp.where(low, c_re2, -c_im2), gm, precision=lax.Precision.HIGHEST,
                 preferred_element_type=F32)
    krow = lax.broadcasted_iota(jnp.int32, (C, C * T), 0)
    klane = lax.broadcasted_iota(jnp.int32, (C, C * T), 1)
    kr = kr + jnp.where(klane == krow * T + (T - 1), dcol_ref[0, :, 0:1], 0.0)

    t_idx = lax.broadcasted_iota(jnp.int32, (T, LANES), 0)
    s_idx = lax.broadcasted_iota(jnp.int32, (T, LANES), 1) & (T - 1)
    causal = s_idx <= t_idx
    for c in range(C):
        for k in range(C * T // LANES):
            z = jnp.broadcast_to(kr[c:c + 1, k * LANES:(k + 1) * LANES], (T, LANES))
            z = pltpu.roll(z, LANES - (T - 1), axis=1, stride=1, stride_axis=0)
            a1_scr[c * T:(c + 1) * T, k * LANES:(k + 1) * LANES] = jnp.where(causal, z, 0.0).astype(BF16)

    dt_r = jnp.exp(rowp_ref[0, 34:35, :])
    steps = (lax.broadcasted_iota(jnp.int32, (T, LANES), 0) + 1).astype(F32)
    pr, pi = _cexp(steps, dt_r * rowp_ref[0, 32:33, :], dt_r * rowp_ref[0, 33:34, :])
    for c in range(C):
        cr, ci = c_re2[c:c + 1, :], c_im2[c:c + 1, :]
        cw_scr[c * T:(c + 1) * T, :] = jnp.where(low, cr * pr - ci * pi, -(cr * pi + ci * pr)).astype(BF16)

    u = u_ref[...].reshape(n_rows, n_cols)
    y1 = jnp.dot(a1_scr[...], u, preferred_element_type=F32)
    xr = y1[n_rows:n_rows + P]
    xi = y1[n_rows + P:]
    kidx = lax.broadcasted_iota(jnp.int32, (P, n_cols), 1) & (chunks_per_seq - 1)
    mr, mi = _cexp(float(T), lam_re, lam_im)
    sh = 1
    while sh < chunks_per_seq:
        rr = pltpu.roll(xr, sh, axis=1)
        ri = pltpu.roll(xi, sh, axis=1)
        ok = kidx >= sh
        xr, xi = (xr + jnp.where(ok, mr * rr - mi * ri, 0.0),
                  xi + jnp.where(ok, mr * ri + mi * rr, 0.0))
        mr, mi = mr * mr - mi * mi, 2.0 * mr * mi
        sh *= 2
    ok = kidx >= 1
    hr = jnp.where(ok, pltpu.roll(xr, 1, axis=1), 0.0)
    hi = jnp.where(ok, pltpu.roll(xi, 1, axis=1), 0.0)
    h = jnp.concatenate([hr, hi], axis=0).astype(BF16)
    y = y1[:n_rows] + jnp.dot(cw_scr[...], h, preferred_element_type=F32)
    y_ref[...] = jax.nn.gelu(y).astype(BF16).reshape(C, T, n_cols)


def _ssm(ut, colp, rowp, dcol, chunks_per_seq):
    n_cols = ut.shape[-1]
    n_rows = SSM_GROUP * SSM_CHUNK
    blk = pl.BlockSpec((SSM_GROUP, SSM_CHUNK, n_cols), lambda g: (g, 0, 0))
    per_group = lambda a: pl.BlockSpec((1,) + a.shape[1:], lambda g: (g, 0, 0))
    return pl.pallas_call(
        functools.partial(_ssm_kernel, chunks_per_seq=chunks_per_seq),
        grid=(N_SSM_GROUPS,),
        in_specs=[blk, per_group(colp), per_group(rowp), per_group(dcol)],
        out_specs=blk,
        out_shape=jax.ShapeDtypeStruct(ut.shape, BF16),
        scratch_shapes=[pltpu.VMEM((n_rows + 2 * STATE_DIM, n_rows), BF16),
                        pltpu.VMEM((n_rows, 2 * STATE_DIM), BF16)],
        compiler_params=pltpu.CompilerParams(dimension_semantics=("arbitrary",)),
        name="ssm",
    )(ut, colp, rowp, dcol)


def _pack_ssm_params(a_re, a_im, log_dt, b_re, b_im, c_re, c_im, d_skip):
    G, P, C = N_SSM_GROUPS, STATE_DIM, SSM_GROUP
    f = lambda t: t.astype(F32)
    log_dt_col = jnp.broadcast_to(f(log_dt)[:, None, None], (G, P, 1))
    colp = jnp.concatenate([f(a_re)[:, :, None], f(a_im)[:, :, None], f(b_re), f(b_im), log_dt_col,
                            jnp.zeros((G, P, LANES - 3 - 2 * C), F32)], axis=2)
    twice = lambda t: jnp.concatenate([f(t), f(t)], axis=-1)
    rowp = jnp.concatenate([twice(c_re), twice(c_im), twice(a_re)[:, None, :], twice(a_im)[:, None, :],
                            jnp.broadcast_to(f(log_dt)[:, None, None], (G, 1, 2 * P)),
                            jnp.zeros((G, 5, 2 * P), F32)], axis=1)
    dcol = jnp.broadcast_to(f(d_skip).reshape(G, C, 1), (G, C, LANES))
    return colp, rowp, dcol


def _mix_ffn2_kernel(x1_ref, o1_ref, o4_ref, o16_ref, l1_ref, l4_ref, l16_ref, z_ref,
                     wglu_ref, bglu_ref, wout_ref, gpost_ref, g3_ref, wg_ref, wu_ref, wo_ref, p3_ref,
                     y_ref, o_scr, l_scr, x2_scr):
    tm = TOKEN_TILE
    step = pl.program_id(0)

    @pl.when(step == 0)
    def _init():
        x2_scr[...] = jnp.zeros(x2_scr.shape, F32)

    prev_slot = lax.rem(step + 1, 2)
    y_ref[...] = _swiglu_ffn(lambda: x2_scr[prev_slot], g3_ref[...], wg_ref, wu_ref, wo_ref, p3_ref[...])

    for n, (d, o_ref, l_ref) in enumerate(((4, o4_ref, l4_ref), (16, o16_ref, l16_ref))):
        for r in range(d):
            for c in range(ATTN_WIDTH // LANES):
                o_scr[n, c, pl.ds(r, tm // d, stride=d), :] = o_ref[0, r, :, c * LANES:(c + 1) * LANES].astype(F32)
            for c in range(STATS_WIDTH // LANES):
                l_scr[n, c, pl.ds(r, tm // d, stride=d), :] = l_ref[0, r, :, c * LANES:(c + 1) * LANES]
    ms = (l1_ref[:, 0:LANES], l_scr[0, 0], l_scr[1, 0])
    ls = (l1_ref[:, LANES:], l_scr[0, 1], l_scr[1, 1])
    m = jnp.maximum(jnp.maximum(ms[0], ms[1]), ms[2])
    es = [jnp.exp2(mn - m) for mn in ms]
    inv = 1.0 / (es[0] * ls[0] + es[1] * ls[1] + es[2] * ls[2])
    ws = [e * inv for e in es]
    low_half = lax.broadcasted_iota(jnp.int32, (tm, 2 * HEAD_DIM), 1) < HEAD_DIM
    pairs = []
    for hp in range(N_HEADS // 2):
        cs = slice(2 * HEAD_DIM * hp, 2 * HEAD_DIM * (hp + 1))
        os_ = (o1_ref[:, cs].astype(F32), o_scr[0, hp], o_scr[1, hp])
        acc = jnp.zeros((tm, 2 * HEAD_DIM), F32)
        for w, o in zip(ws, os_):
            wexp = jnp.where(low_half, w[:, hp:hp + 1], w[:, HEAD_DIM + hp:HEAD_DIM + hp + 1])
            acc = acc + wexp * o
        pairs.append(acc)
    attn = jnp.concatenate(pairs, axis=-1).astype(BF16)
    z = z_ref[...]
    gate = jax.nn.sigmoid(jnp.dot(z, wglu_ref[...], preferred_element_type=F32) + bglu_ref[...])
    ssm = (z.astype(F32) * gate).astype(BF16)
    mixed = (jnp.dot(attn, wout_ref[0:ATTN_WIDTH, :], preferred_element_type=F32)
             + jnp.dot(ssm, wout_ref[ATTN_WIDTH:, :], preferred_element_type=F32))
    x2_scr[lax.rem(step, 2)] = x1_ref[...] + _rms(mixed, gpost_ref[...])


def _mix_ffn2(x1, o1, o4, o16, l1, l4, l16, z, wglu, bglu, wout, gpost, g3, wg, wu, wo, p3, seq_len):
    rows = x1.shape[0]
    tm = TOKEN_TILE
    n_tiles = rows // tm
    tok_now, res_now = _token_specs(n_tiles, seq_len // tm, 0)
    tok_prev, _ = _token_specs(n_tiles, seq_len // tm, 1)
    return pl.pallas_call(
        _mix_ffn2_kernel,
        grid=(n_tiles + 1,),
        in_specs=[tok_now(D_MODEL), tok_now(ATTN_WIDTH), res_now(4, ATTN_WIDTH), res_now(16, ATTN_WIDTH),
                  tok_now(STATS_WIDTH), res_now(4, STATS_WIDTH), res_now(16, STATS_WIDTH), tok_now(SSM_WIDTH),
                  _const_spec((SSM_WIDTH, SSM_WIDTH)), _const_spec((1, SSM_WIDTH)),
                  _const_spec((D_MODEL, D_MODEL)), _const_spec((1, D_MODEL)), _const_spec((1, D_MODEL)),
                  _const_spec((D_MODEL, D_FF)), _const_spec((D_MODEL, D_FF)), _const_spec((D_FF, D_MODEL)),
                  _const_spec((1, D_MODEL))],
        out_specs=tok_prev(D_MODEL),
        out_shape=jax.ShapeDtypeStruct((rows, D_MODEL), F32),
        scratch_shapes=[pltpu.VMEM((2, ATTN_WIDTH // LANES, tm, LANES), F32),
                        pltpu.VMEM((2, STATS_WIDTH // LANES, tm, LANES), F32),
                        pltpu.VMEM((2, tm, D_MODEL), F32)],
        compiler_params=pltpu.CompilerParams(
            dimension_semantics=("arbitrary",), vmem_limit_bytes=VMEM_LIMIT_BYTES),
        name="mix_ffn2",
    )(x1, o1, o4, o16, l1, l4, l16, z, wglu, bglu, wout, gpost, g3, wg, wu, wo, p3)


def _split_ffn_weights(w_in, w_out):
    return w_in[:, :D_FF].astype(BF16), w_in[:, D_FF:].astype(BF16), w_out.astype(BF16)


def _row(v):
    return v.astype(F32).reshape(1, -1)


def kernel(x, ffn1_pre_g, ffn1_w_in, ffn1_w_out, ffn1_post_g, mix_pre_g, w_mix_in, a_re, a_im, log_dt, b_re, b_im, c_re, c_im, d_skip, w_glu, b_glu, w_mix_out, mix_post_g, ffn2_pre_g, ffn2_w_in, ffn2_w_out, ffn2_post_g):
    B, S, _ = x.shape
    depth = ffn1_pre_g.shape[0]
    n_chunks = S // SSM_CHUNK
    for l in range(depth):
        wg1, wu1, wo1 = _split_ffn_weights(ffn1_w_in[l], ffn1_w_out[l])
        wg2, wu2, wo2 = _split_ffn_weights(ffn2_w_in[l], ffn2_w_out[l])
        (x1, q1, k1, v1, q4, k4, v4, q16, k16, v16, u) = _ffn1_proj(
            x, _row(ffn1_pre_g[l]), wg1, wu1, wo1, _row(ffn1_post_g[l]), _row(mix_pre_g[l]),
            w_mix_in[l].astype(BF16))

        outs = []
        for d, (q, k, v) in zip(DILATIONS, ((q1, k1, v1), (q4, k4, v4), (q16, k16, v16))):
            flat = lambda t: t.reshape(B * S, ATTN_WIDTH)
            outs.append(_attention_branch(flat(q), flat(k), flat(v), S // d, d))
        (o1, l1), (o4, l4), (o16, l16) = outs
        o4 = o4.reshape(B, 4, S // 4, ATTN_WIDTH)
        l4 = l4.reshape(B, 4, S // 4, STATS_WIDTH)
        o16 = o16.reshape(B, 16, S // 16, ATTN_WIDTH)
        l16 = l16.reshape(B, 16, S // 16, STATS_WIDTH)

        colp, rowp, dcol = _pack_ssm_params(a_re[l], a_im[l], log_dt[l], b_re[l], b_im[l], c_re[l], c_im[l],
                                            d_skip[l])
        ut = u.reshape(B, n_chunks, SSM_CHUNK, SSM_WIDTH).transpose(3, 2, 0, 1)
        ut = ut.reshape(SSM_WIDTH, SSM_CHUNK, B * n_chunks)
        zt = _ssm(ut, colp, rowp, dcol, n_chunks)
        z = zt.reshape(SSM_WIDTH, SSM_CHUNK, B, n_chunks).transpose(2, 3, 1, 0).reshape(B * S, SSM_WIDTH)

        x = _mix_ffn2(x1, o1, o4, o16, l1, l4, l16, z, w_glu[l].astype(BF16), _row(b_glu[l]),
                      w_mix_out[l].astype(BF16), _row(mix_post_g[l]), _row(ffn2_pre_g[l]), wg2, wu2, wo2,
                      _row(ffn2_post_g[l]), S).reshape(B, S, D_MODEL)
    return x
```

```python
import functools

import jax
import jax.numpy as jnp
from jax import lax
from jax.experimental import pallas as pl
from jax.experimental.pallas import tpu as pltpu

F32 = jnp.float32
BF16 = jnp.bfloat16

D_MODEL = 1024
ATTN_WIDTH = 512
SSM_WIDTH = 512
HEAD_DIM = 64
N_HEADS = 8
DILATIONS = (1, 4, 16)
WINDOW_STEPS = 128
QBLK = 128
SSM_GROUP = 16
N_SSM_GROUPS = 32
STATE_DIM = 64
D_FF = 2816
NORM_EPS = 1e-6

TOKEN_TILE = 512
FF_CHUNKS = (256,) * 11
SSM_CHUNK = 64
SSM_GROUPS_PER_STEP = 2
ATTN_TILE = 2048
MASK_VALUE = float("-inf")
LANES = 128
LOG2_E = 1.4426950408889634
QK_SCALE_LOG2 = HEAD_DIM ** -0.5 * LOG2_E
STATS_WIDTH = 4 * HEAD_DIM
PROJ_SLABS = 4 * ATTN_WIDTH // LANES
VMEM_LIMIT_BYTES = 56 * 1024 * 1024


def _rms(x, g):
    return x * lax.rsqrt(jnp.mean(x * x, axis=-1, keepdims=True) + NORM_EPS) * g


def _swiglu_chunks(h, acc, chunks, wg_ref, wu_ref, wo_ref):
    for start, width in chunks:
        sl = slice(start, start + width)
        gate = jnp.dot(h, wg_ref[:, sl], preferred_element_type=F32)
        up = jnp.dot(h, wu_ref[:, sl], preferred_element_type=F32)
        act = (gate * jax.nn.sigmoid(gate) * up).astype(BF16)
        acc = acc + jnp.dot(act, wo_ref[sl, :], preferred_element_type=F32)
    return acc


def _ff_chunks():
    starts = [sum(FF_CHUNKS[:j]) for j in range(len(FF_CHUNKS))]
    return list(zip(starts, FF_CHUNKS))


def _swiglu_ffn(read_x, h, wg_ref, wu_ref, wo_ref, post_g):
    acc = _swiglu_chunks(h, jnp.zeros((h.shape[0], D_MODEL), F32), _ff_chunks(), wg_ref, wu_ref, wo_ref)
    return read_x() + 0.5 * _rms(acc, post_g)


def _const_spec(shape):
    return pl.BlockSpec(shape, lambda *_: (0,) * len(shape), pipeline_mode=pl.Buffered(1))


def _token_specs(n_tiles, tiles_per_seq, lag):
    tm = TOKEN_TILE

    def tile(i):
        return jnp.clip(i - lag, 0, n_tiles - 1)

    def tok(w):
        return pl.BlockSpec((tm, w), lambda i: (tile(i), 0))

    def res(d, w):
        return pl.BlockSpec((1, d, tm // d, w),
                            lambda i: (tile(i) // tiles_per_seq, 0, tile(i) % tiles_per_seq, 0))

    return tok, res


def _ffn1_proj_kernel(x_ref, g1_ref, wg_ref, wu_ref, wo_ref, p1_ref, gm_ref, wm_ref,
                      x1_ref, q1_ref, k1_ref, v1_ref, q4_ref, k4_ref, v4_ref,
                      q16_ref, k16_ref, v16_ref, u_ref, proj_scr, mod4_scr):
    @pl.when(pl.program_id(0) == 0)
    def _init():
        proj_scr[...] = jnp.zeros(proj_scr.shape, F32)

    lanes_per = ATTN_WIDTH // LANES
    q4_rows = TOKEN_TILE // 4
    outs = ((q1_ref, q4_ref, q16_ref), (k1_ref, k4_ref, k16_ref), (v1_ref, v4_ref, v16_ref))
    for t, (o1, o4, o16) in enumerate(outs):
        for c in range(lanes_per):
            cs = slice(c * LANES, (c + 1) * LANES)
            slab = t * lanes_per + c
            o1[:, cs] = proj_scr[slab].astype(BF16)
            for b in range(4):
                cls4 = proj_scr[slab, pl.ds(b, q4_rows, stride=4), :]
                o4[0, b, :, cs] = cls4.astype(BF16)
                mod4_scr[slab, b * q4_rows:(b + 1) * q4_rows, :] = cls4
            for b in range(4):
                for a in range(4):
                    o16[0, 4 * a + b, :, cs] = mod4_scr[
                        slab, pl.ds(b * q4_rows + a, q4_rows // 4, stride=4), :].astype(BF16)
    for c in range(lanes_per):
        u_ref[:, c * LANES:(c + 1) * LANES] = proj_scr[3 * lanes_per + c].astype(BF16)

    h1 = _rms(x_ref[...], g1_ref[...]).astype(BF16)
    x1 = _swiglu_ffn(lambda: x_ref[...], h1, wg_ref, wu_ref, wo_ref, p1_ref[...])
    x1_ref[...] = x1
    h = _rms(x1, gm_ref[...]).astype(BF16)
    proj = jnp.dot(h, wm_ref[...], preferred_element_type=F32)
    for cb in range(PROJ_SLABS):
        slab = proj[:, cb * LANES:(cb + 1) * LANES]
        proj_scr[cb] = slab * QK_SCALE_LOG2 if cb < lanes_per else slab


def _ffn1_proj(x, g1, wg, wu, wo, p1, gm, wm):
    B, S, _ = x.shape
    tm = TOKEN_TILE
    n_tiles = B * S // tm
    tok_now, _ = _token_specs(n_tiles, S // tm, 0)
    tok_prev, res_prev = _token_specs(n_tiles, S // tm, 1)
    nat = jax.ShapeDtypeStruct((B * S, ATTN_WIDTH), BF16)
    r4 = jax.ShapeDtypeStruct((B, 4, S // 4, ATTN_WIDTH), BF16)
    r16 = jax.ShapeDtypeStruct((B, 16, S // 16, ATTN_WIDTH), BF16)
    return pl.pallas_call(
        _ffn1_proj_kernel,
        grid=(n_tiles + 1,),
        in_specs=[tok_now(D_MODEL), _const_spec((1, D_MODEL)),
                  _const_spec((D_MODEL, D_FF)), _const_spec((D_MODEL, D_FF)), _const_spec((D_FF, D_MODEL)),
                  _const_spec((1, D_MODEL)), _const_spec((1, D_MODEL)), _const_spec((D_MODEL, 4 * ATTN_WIDTH))],
        out_specs=[tok_now(D_MODEL)] + [tok_prev(ATTN_WIDTH)] * 3 + [res_prev(4, ATTN_WIDTH)] * 3
        + [res_prev(16, ATTN_WIDTH)] * 3 + [tok_prev(SSM_WIDTH)],
        out_shape=[jax.ShapeDtypeStruct((B * S, D_MODEL), F32)] + [nat] * 3 + [r4] * 3 + [r16] * 3 + [nat],
        scratch_shapes=[pltpu.VMEM((PROJ_SLABS, tm, LANES), F32),
                        pltpu.VMEM((3 * ATTN_WIDTH // LANES, tm, LANES), F32)],
        compiler_params=pltpu.CompilerParams(
            dimension_semantics=("arbitrary",), vmem_limit_bytes=VMEM_LIMIT_BYTES),
        name="ffn1_proj",
    )(x.reshape(B * S, D_MODEL), g1, wg, wu, wo, p1, gm, wm)


def _attn_kernel(q_ref, kc_ref, kp_ref, vc_ref, vp_ref, o_ref, st_ref, bias_ref, *, blocks_per_seq, dilation):
    @pl.when(pl.program_id(0) == 0)
    def _build_bias_tables():
        qi = lax.broadcasted_iota(jnp.int32, (QBLK, 2 * QBLK), 0)
        ci = lax.broadcasted_iota(jnp.int32, (QBLK, 2 * QBLK), 1)
        steps = QBLK + qi - ci
        dist = (steps * dilation).astype(F32)
        band = jnp.where(steps >= 0, jnp.where(steps <= WINDOW_STEPS, 1, 0), 0)
        band_first = jnp.where(ci >= QBLK, band, 0)
        for h in range(N_HEADS):
            bias = -(2.0 ** (-8.0 * (h + 1) / N_HEADS) * LOG2_E) * dist
            bias_ref[0, h] = jnp.where(band == 1, bias, MASK_VALUE)
            bias_ref[1, h] = jnp.where(band_first == 1, bias, MASK_VALUE)

    n_blocks = ATTN_TILE // QBLK
    lane = lax.broadcasted_iota(jnp.int32, (1, 2 * HEAD_DIM), 1)
    head_mask = ((lane < HEAD_DIM).astype(BF16), (lane >= HEAD_DIM).astype(BF16))
    ones_cols = tuple(jnp.broadcast_to(mk, (2 * QBLK, 2 * HEAD_DIM)) for mk in head_mask)
    lane_f = lax.broadcasted_iota(jnp.int32, (QBLK, 2 * HEAD_DIM), 1)
    low_half = lane_f < HEAD_DIM

    for j in range(n_blocks):
        rows = slice(j * QBLK, (j + 1) * QBLK)
        if blocks_per_seq >= n_blocks:
            tiles_per_seq = blocks_per_seq // n_blocks
            table = (lax.rem(pl.program_id(0), tiles_per_seq) == 0).astype(jnp.int32) if j == 0 else 0
        else:
            table = 1 if j % blocks_per_seq == 0 else 0
        m_tile = jnp.zeros((QBLK, 2 * HEAD_DIM), F32)
        l_tile = jnp.ones((QBLK, 2 * HEAD_DIM), F32)
        for hp in range(N_HEADS // 2):
            cs = slice(2 * HEAD_DIM * hp, 2 * HEAD_DIM * (hp + 1))
            q = q_ref[rows, cs]
            if j == 0:
                kk = jnp.concatenate([kp_ref[:, cs], kc_ref[0:QBLK, cs]], axis=0)
                vv = jnp.concatenate([vp_ref[:, cs], vc_ref[0:QBLK, cs]], axis=0)
            else:
                kk = kc_ref[(j - 1) * QBLK:(j + 1) * QBLK, cs]
                vv = vc_ref[(j - 1) * QBLK:(j + 1) * QBLK, cs]
            ps, ms, vas = [], [], []
            for e in range(2):
                s = lax.dot_general(q * head_mask[e], kk, (((1,), (1,)), ((), ())),
                                    preferred_element_type=F32)
                s = s + bias_ref[table, 2 * hp + e]
                m = jnp.max(s, axis=-1, keepdims=True)
                ps.append(jnp.exp2(s - m).astype(BF16))
                ms.append(m)
                vas.append(jnp.concatenate([vv * head_mask[e], ones_cols[e]], axis=1))
            pv = jnp.dot(jnp.concatenate(ps, axis=1), jnp.concatenate(vas, axis=0),
                         preferred_element_type=F32)
            o_ref[rows, cs] = pv[:, :2 * HEAD_DIM].astype(BF16)
            mine = (lane_f & (HEAD_DIM - 1)) == hp
            m_tile = jnp.where(mine, jnp.where(low_half, ms[0], ms[1]), m_tile)
            l_tile = jnp.where(mine, pv[:, 2 * HEAD_DIM:], l_tile)
        st_ref[rows, 0:2 * HEAD_DIM] = m_tile
        st_ref[rows, 2 * HEAD_DIM:] = l_tile


def _attention_branch(q, k, v, seq_len, dilation):
    rows = q.shape[0]
    ratio = ATTN_TILE // QBLK
    cur = pl.BlockSpec((ATTN_TILE, ATTN_WIDTH), lambda i: (i, 0))
    prev = pl.BlockSpec((QBLK, ATTN_WIDTH), lambda i: (jnp.maximum(i * ratio - 1, 0), 0))
    return pl.pallas_call(
        functools.partial(_attn_kernel, blocks_per_seq=seq_len // QBLK, dilation=dilation),
        grid=(rows // ATTN_TILE,),
        in_specs=[cur, cur, prev, cur, prev],
        out_specs=[cur, pl.BlockSpec((ATTN_TILE, STATS_WIDTH), lambda i: (i, 0))],
        out_shape=[jax.ShapeDtypeStruct((rows, ATTN_WIDTH), BF16),
                   jax.ShapeDtypeStruct((rows, STATS_WIDTH), F32)],
        scratch_shapes=[pltpu.VMEM((2, N_HEADS, QBLK, 2 * QBLK), F32)],
        compiler_params=pltpu.CompilerParams(dimension_semantics=("arbitrary",)),
        name=f"attn_s{seq_len}",
    )(q, k, k, v, v)


def _cexp(n, lam_re, lam_im):
    mag = jnp.exp(n * lam_re)
    return mag * jnp.cos(n * lam_im), mag * jnp.sin(n * lam_im)


def _ssm_kernel(u_ref, colp_ref, rowp_ref, dcol_ref, y_ref, a1_scr, cw_scr, *, chunks_per_seq):
    for gi in range(SSM_GROUPS_PER_STEP):
        rows = pl.ds(gi * SSM_GROUP, SSM_GROUP)
        _ssm_group(u_ref.at[rows], colp_ref.at[gi], rowp_ref.at[gi], dcol_ref.at[gi], y_ref.at[rows],
                   a1_scr.at[gi], cw_scr.at[gi], chunks_per_seq)


def _ssm_group(u_ref, colp_ref, rowp_ref, dcol_ref, y_ref, a1_scr, cw_scr, chunks_per_seq):
    T, P, C = SSM_CHUNK, STATE_DIM, SSM_GROUP
    n_rows = C * T
    n_cols = u_ref.shape[-1]
    lane = lax.broadcasted_iota(jnp.int32, (1, LANES), 1)
    low = lane < P

    a_re = colp_ref[:, 0:1]
    a_im = colp_ref[:, 1:2]
    dt = jnp.exp(colp_ref[:, 34:35])
    lam_re, lam_im = dt * a_re, dt * a_im
    ab_re, ab_im = _cexp(1.0, lam_re, lam_im)
    inv_a2 = 1.0 / (a_re * a_re + a_im * a_im)
    nr, ni = ab_re - 1.0, ab_im
    cf_re = (nr * a_re + ni * a_im) * inv_a2
    cf_im = (ni * a_re - nr * a_im) * inv_a2

    rev = (T - 1 - (lane & (T - 1))).astype(F32)
    pw_re, pw_im = _cexp(rev, lam_re, lam_im)
    g_re, g_im = [], []
    for k in range(C // 2):
        b_re = jnp.where(low, colp_ref[:, 2 + 2 * k:3 + 2 * k], colp_ref[:, 3 + 2 * k:4 + 2 * k])
        b_im = jnp.where(low, colp_ref[:, 18 + 2 * k:19 + 2 * k], colp_ref[:, 19 + 2 * k:20 + 2 * k])
        bb_re = cf_re * b_re - cf_im * b_im
        bb_im = cf_re * b_im + cf_im * b_re
        g_re.append(pw_re * bb_re - pw_im * bb_im)
        g_im.append(pw_re * bb_im + pw_im * bb_re)
    gm = jnp.concatenate([jnp.concatenate(g_re, axis=1), jnp.concatenate(g_im, axis=1)], axis=0)
    a1_scr[n_rows:, :] = gm.astype(BF16)

    c_re2 = rowp_ref[0:C, :]
    c_im2 = rowp_ref[C:2 * C, :]
    kr = jnp.dot(jnp.where(low, c_re2, -c_im2), gm, precision=lax.Precision.HIGHEST,
                 preferred_element_type=F32)
    krow = lax.broadcasted_iota(jnp.int32, (C, C * T), 0)
    klane = lax.broadcasted_iota(jnp.int32, (C, C * T), 1)
    kr = kr + jnp.where(klane == krow * T + (T - 1), dcol_ref[:, 0:1], 0.0)

    t_idx = lax.broadcasted_iota(jnp.int32, (T, LANES), 0)
    s_idx = lax.broadcasted_iota(jnp.int32, (T, LANES), 1) & (T - 1)
    causal = s_idx <= t_idx
    for c in range(C):
        for k in range(C * T // LANES):
            z = jnp.broadcast_to(kr[c:c + 1, k * LANES:(k + 1) * LANES], (T, LANES))
            z = pltpu.roll(z, LANES - (T - 1), axis=1, stride=1, stride_axis=0)
            a1_scr[c * T:(c + 1) * T, k * LANES:(k + 1) * LANES] = jnp.where(causal, z, 0.0).astype(BF16)

    dt_r = jnp.exp(rowp_ref[34:35, :])
    steps = (lax.broadcasted_iota(jnp.int32, (T, LANES), 0) + 1).astype(F32)
    pr, pi = _cexp(steps, dt_r * rowp_ref[32:33, :], dt_r * rowp_ref[33:34, :])
    for c in range(C):
        cr, ci = c_re2[c:c + 1, :], c_im2[c:c + 1, :]
        cw_scr[c * T:(c + 1) * T, :] = jnp.where(low, cr * pr - ci * pi, -(cr * pi + ci * pr)).astype(BF16)

    u = u_ref[...].reshape(n_rows, n_cols)
    y1 = jnp.dot(a1_scr[...], u, preferred_element_type=F32)
    xr = y1[n_rows:n_rows + P]
    xi = y1[n_rows + P:]
    kidx = lax.broadcasted_iota(jnp.int32, (P, n_cols), 1) & (chunks_per_seq - 1)
    mr, mi = _cexp(float(T), lam_re, lam_im)
    sh = 1
    while sh < chunks_per_seq:
        rr = pltpu.roll(xr, sh, axis=1)
        ri = pltpu.roll(xi, sh, axis=1)
        ok = kidx >= sh
        xr, xi = (xr + jnp.where(ok, mr * rr - mi * ri, 0.0),
                  xi + jnp.where(ok, mr * ri + mi * rr, 0.0))
        mr, mi = mr * mr - mi * mi, 2.0 * mr * mi
        sh *= 2
    ok = kidx >= 1
    hr = jnp.where(ok, pltpu.roll(xr, 1, axis=1), 0.0)
    hi = jnp.where(ok, pltpu.roll(xi, 1, axis=1), 0.0)
    h = jnp.concatenate([hr, hi], axis=0).astype(BF16)
    y = y1[:n_rows] + jnp.dot(cw_scr[...], h, preferred_element_type=F32)
    y_ref[...] = jax.nn.gelu(y).astype(BF16).reshape(C, T, n_cols)


def _ssm(ut, colp, rowp, dcol, chunks_per_seq):
    n_cols = ut.shape[-1]
    n_rows = SSM_GROUP * SSM_CHUNK
    per_step = SSM_GROUPS_PER_STEP
    blk = pl.BlockSpec((per_step * SSM_GROUP, SSM_CHUNK, n_cols), lambda g: (g, 0, 0))
    per_group = lambda a: pl.BlockSpec((per_step,) + a.shape[1:], lambda g: (g, 0, 0))
    return pl.pallas_call(
        functools.partial(_ssm_kernel, chunks_per_seq=chunks_per_seq),
        grid=(N_SSM_GROUPS // per_step,),
        in_specs=[blk, per_group(colp), per_group(rowp), per_group(dcol)],
        out_specs=blk,
        out_shape=jax.ShapeDtypeStruct(ut.shape, BF16),
        scratch_shapes=[pltpu.VMEM((per_step, n_rows + 2 * STATE_DIM, n_rows), BF16),
                        pltpu.VMEM((per_step, n_rows, 2 * STATE_DIM), BF16)],
        compiler_params=pltpu.CompilerParams(dimension_semantics=("arbitrary",)),
        name="ssm",
    )(ut, colp, rowp, dcol)


def _pack_ssm_params(a_re, a_im, log_dt, b_re, b_im, c_re, c_im, d_skip):
    G, P, C = N_SSM_GROUPS, STATE_DIM, SSM_GROUP
    f = lambda t: t.astype(F32)
    log_dt_col = jnp.broadcast_to(f(log_dt)[:, None, None], (G, P, 1))
    colp = jnp.concatenate([f(a_re)[:, :, None], f(a_im)[:, :, None], f(b_re), f(b_im), log_dt_col,
                            jnp.zeros((G, P, LANES - 3 - 2 * C), F32)], axis=2)
    twice = lambda t: jnp.concatenate([f(t), f(t)], axis=-1)
    rowp = jnp.concatenate([twice(c_re), twice(c_im), twice(a_re)[:, None, :], twice(a_im)[:, None, :],
                            jnp.broadcast_to(f(log_dt)[:, None, None], (G, 1, 2 * P)),
                            jnp.zeros((G, 5, 2 * P), F32)], axis=1)
    dcol = jnp.broadcast_to(f(d_skip).reshape(G, C, 1), (G, C, LANES))
    return colp, rowp, dcol


def _mix_ffn2_kernel(x1_ref, o1_ref, o4_ref, o16_ref, l1_ref, l4_ref, l16_ref, z_ref,
                     wglu_ref, bglu_ref, wout_ref, gpost_ref, g3_ref, wg_ref, wu_ref, wo_ref, p3_ref,
                     y_ref, o_scr, l_scr, x2_scr):
    tm = TOKEN_TILE
    step = pl.program_id(0)

    @pl.when(step == 0)
    def _init():
        x2_scr[...] = jnp.zeros(x2_scr.shape, F32)

    prev_slot = lax.rem(step + 1, 2)
    h3 = _rms(x2_scr[prev_slot], g3_ref[...]).astype(BF16)
    y_ref[...] = _swiglu_ffn(lambda: x2_scr[prev_slot], h3, wg_ref, wu_ref, wo_ref, p3_ref[...])

    for n, (d, o_ref, l_ref) in enumerate(((4, o4_ref, l4_ref), (16, o16_ref, l16_ref))):
        for r in range(d):
            for c in range(ATTN_WIDTH // LANES):
                o_scr[n, c, pl.ds(r, tm // d, stride=d), :] = o_ref[0, r, :, c * LANES:(c + 1) * LANES].astype(F32)
            for c in range(STATS_WIDTH // LANES):
                l_scr[n, c, pl.ds(r, tm // d, stride=d), :] = l_ref[0, r, :, c * LANES:(c + 1) * LANES]
    ms = (l1_ref[:, 0:LANES], l_scr[0, 0], l_scr[1, 0])
    ls = (l1_ref[:, LANES:], l_scr[0, 1], l_scr[1, 1])
    m = jnp.maximum(jnp.maximum(ms[0], ms[1]), ms[2])
    es = [jnp.exp2(mn - m) for mn in ms]
    inv = 1.0 / (es[0] * ls[0] + es[1] * ls[1] + es[2] * ls[2])
    ws = [e * inv for e in es]
    low_half = lax.broadcasted_iota(jnp.int32, (tm, 2 * HEAD_DIM), 1) < HEAD_DIM
    pairs = []
    for hp in range(N_HEADS // 2):
        cs = slice(2 * HEAD_DIM * hp, 2 * HEAD_DIM * (hp + 1))
        os_ = (o1_ref[:, cs].astype(F32), o_scr[0, hp], o_scr[1, hp])
        acc = jnp.zeros((tm, 2 * HEAD_DIM), F32)
        for w, o in zip(ws, os_):
            wexp = jnp.where(low_half, w[:, hp:hp + 1], w[:, HEAD_DIM + hp:HEAD_DIM + hp + 1])
            acc = acc + wexp * o
        pairs.append(acc)
    attn = jnp.concatenate(pairs, axis=-1).astype(BF16)
    z = z_ref[...]
    gate = jax.nn.sigmoid(jnp.dot(z, wglu_ref[...], preferred_element_type=F32) + bglu_ref[...])
    ssm = (z.astype(F32) * gate).astype(BF16)
    mixed = (jnp.dot(attn, wout_ref[0:ATTN_WIDTH, :], preferred_element_type=F32)
             + jnp.dot(ssm, wout_ref[ATTN_WIDTH:, :], preferred_element_type=F32))
    x2_scr[lax.rem(step, 2)] = x1_ref[...] + _rms(mixed, gpost_ref[...])


def _mix_ffn2(x1, o1, o4, o16, l1, l4, l16, z, wglu, bglu, wout, gpost, g3, wg, wu, wo, p3, seq_len):
    rows = x1.shape[0]
    tm = TOKEN_TILE
    n_tiles = rows // tm
    tok_now, res_now = _token_specs(n_tiles, seq_len // tm, 0)
    tok_prev, _ = _token_specs(n_tiles, seq_len // tm, 1)
    return pl.pallas_call(
        _mix_ffn2_kernel,
        grid=(n_tiles + 1,),
        in_specs=[tok_now(D_MODEL), tok_now(ATTN_WIDTH), res_now(4, ATTN_WIDTH), res_now(16, ATTN_WIDTH),
                  tok_now(STATS_WIDTH), res_now(4, STATS_WIDTH), res_now(16, STATS_WIDTH), tok_now(SSM_WIDTH),
                  _const_spec((SSM_WIDTH, SSM_WIDTH)), _const_spec((1, SSM_WIDTH)),
                  _const_spec((D_MODEL, D_MODEL)), _const_spec((1, D_MODEL)), _const_spec((1, D_MODEL)),
                  _const_spec((D_MODEL, D_FF)), _const_spec((D_MODEL, D_FF)), _const_spec((D_FF, D_MODEL)),
                  _const_spec((1, D_MODEL))],
        out_specs=tok_prev(D_MODEL),
        out_shape=jax.ShapeDtypeStruct((rows, D_MODEL), F32),
        scratch_shapes=[pltpu.VMEM((2, ATTN_WIDTH // LANES, tm, LANES), F32),
                        pltpu.VMEM((2, STATS_WIDTH // LANES, tm, LANES), F32),
                        pltpu.VMEM((2, tm, D_MODEL), F32)],
        compiler_params=pltpu.CompilerParams(
            dimension_semantics=("arbitrary",), vmem_limit_bytes=VMEM_LIMIT_BYTES),
        name="mix_ffn2",
    )(x1, o1, o4, o16, l1, l4, l16, z, wglu, bglu, wout, gpost, g3, wg, wu, wo, p3)


def _split_ffn_weights(w_in, w_out):
    return w_in[:, :D_FF].astype(BF16), w_in[:, D_FF:].astype(BF16), w_out.astype(BF16)


def _row(v):
    return v.astype(F32).reshape(1, -1)


def kernel(x, ffn1_pre_g, ffn1_w_in, ffn1_w_out, ffn1_post_g, mix_pre_g, w_mix_in, a_re, a_im, log_dt, b_re, b_im, c_re, c_im, d_skip, w_glu, b_glu, w_mix_out, mix_post_g, ffn2_pre_g, ffn2_w_in, ffn2_w_out, ffn2_post_g):
    B, S, _ = x.shape
    depth = ffn1_pre_g.shape[0]
    n_chunks = S // SSM_CHUNK
    for l in range(depth):
        wg1, wu1, wo1 = _split_ffn_weights(ffn1_w_in[l], ffn1_w_out[l])
        wg2, wu2, wo2 = _split_ffn_weights(ffn2_w_in[l], ffn2_w_out[l])
        (x1, q1, k1, v1, q4, k4, v4, q16, k16, v16, u) = _ffn1_proj(
            x, _row(ffn1_pre_g[l]), wg1, wu1, wo1, _row(ffn1_post_g[l]), _row(mix_pre_g[l]),
            w_mix_in[l].astype(BF16))

        outs = []
        for d, (q, k, v) in zip(DILATIONS, ((q1, k1, v1), (q4, k4, v4), (q16, k16, v16))):
            flat = lambda t: t.reshape(B * S, ATTN_WIDTH)
            outs.append(_attention_branch(flat(q), flat(k), flat(v), S // d, d))
        (o1, l1), (o4, l4), (o16, l16) = outs
        o4 = o4.reshape(B, 4, S // 4, ATTN_WIDTH)
        l4 = l4.reshape(B, 4, S // 4, STATS_WIDTH)
        o16 = o16.reshape(B, 16, S // 16, ATTN_WIDTH)
        l16 = l16.reshape(B, 16, S // 16, STATS_WIDTH)

        colp, rowp, dcol = _pack_ssm_params(a_re[l], a_im[l], log_dt[l], b_re[l], b_im[l], c_re[l], c_im[l],
                                            d_skip[l])
        ut = u.reshape(B, n_chunks, SSM_CHUNK, SSM_WIDTH).transpose(3, 2, 0, 1)
        ut = ut.reshape(SSM_WIDTH, SSM_CHUNK, B * n_chunks)
        zt = _ssm(ut, colp, rowp, dcol, n_chunks)
        z = zt.reshape(SSM_WIDTH, SSM_CHUNK, B, n_chunks).transpose(2, 3, 1, 0).reshape(B * S, SSM_WIDTH)

        x = _mix_ffn2(x1, o1, o4, o16, l1, l4, l16, z, w_glu[l].astype(BF16), _row(b_glu[l]),
                      w_mix_out[l].astype(BF16), _row(mix_post_g[l]), _row(ffn2_pre_g[l]), wg2, wu2, wo2,
                      _row(ffn2_post_g[l]), S).reshape(B, S, D_MODEL)
    return x
```

```python
import functools

import jax
import jax.numpy as jnp
from jax import lax
from jax.experimental import pallas as pl
from jax.experimental.pallas import tpu as pltpu

F32 = jnp.float32
BF16 = jnp.bfloat16

D_MODEL = 1024
ATTN_WIDTH = 512
SSM_WIDTH = 512
HEAD_DIM = 64
N_HEADS = 8
DILATIONS = (1, 4, 16)
WINDOW_STEPS = 128
QBLK = 128
SSM_GROUP = 16
N_SSM_GROUPS = 32
STATE_DIM = 64
D_FF = 2816
NORM_EPS = 1e-6

TOKEN_TILE = 512
FF_CHUNKS = (256,) * 11
SSM_CHUNK = 64
SSM_GROUPS_PER_STEP = 2
ATTN_TILE = 2048
MASK_VALUE = float("-inf")
LANES = 128
LOG2_E = 1.4426950408889634
QK_SCALE_LOG2 = HEAD_DIM ** -0.5 * LOG2_E
STATS_WIDTH = 4 * HEAD_DIM
PROJ_SLABS = 4 * ATTN_WIDTH // LANES
VMEM_LIMIT_BYTES = 56 * 1024 * 1024


def _rms(x, g):
    return x * lax.rsqrt(jnp.mean(x * x, axis=-1, keepdims=True) + NORM_EPS) * g


def _swiglu_chunks(h, acc, chunks, wg_ref, wu_ref, wo_ref):
    for start, width in chunks:
        sl = slice(start, start + width)
        gate = jnp.dot(h, wg_ref[:, sl], preferred_element_type=F32)
        up = jnp.dot(h, wu_ref[:, sl], preferred_element_type=F32)
        act = (gate * jax.nn.sigmoid(gate) * up).astype(BF16)
        acc = acc + jnp.dot(act, wo_ref[sl, :], preferred_element_type=F32)
    return acc


def _ff_chunks():
    starts = [sum(FF_CHUNKS[:j]) for j in range(len(FF_CHUNKS))]
    return list(zip(starts, FF_CHUNKS))


def _swiglu_ffn(read_x, h, wg_ref, wu_ref, wo_ref, post_g):
    acc = _swiglu_chunks(h, jnp.zeros((h.shape[0], D_MODEL), F32), _ff_chunks(), wg_ref, wu_ref, wo_ref)
    return read_x() + 0.5 * _rms(acc, post_g)


def _const_spec(shape):
    return pl.BlockSpec(shape, lambda *_: (0,) * len(shape), pipeline_mode=pl.Buffered(1))


def _token_specs(n_tiles, tiles_per_seq, lag):
    tm = TOKEN_TILE

    def tile(i):
        return jnp.clip(i - lag, 0, n_tiles - 1)

    def tok(w):
        return pl.BlockSpec((tm, w), lambda i: (tile(i), 0))

    def res(d, w):
        return pl.BlockSpec((1, d, tm // d, w),
                            lambda i: (tile(i) // tiles_per_seq, 0, tile(i) % tiles_per_seq, 0))

    return tok, res


def _ffn1_kernel(x_ref, g1_ref, wg_ref, wu_ref, wo_ref, p1_ref, gm_ref, x1_ref, h2_ref):
    h1 = _rms(x_ref[...], g1_ref[...]).astype(BF16)
    x1 = _swiglu_ffn(lambda: x_ref[...], h1, wg_ref, wu_ref, wo_ref, p1_ref[...])
    x1_ref[...] = x1
    h2_ref[...] = _rms(x1, gm_ref[...]).astype(BF16)


def _ffn1(x2d, g1, wg, wu, wo, p1, gm):
    rows = x2d.shape[0]
    tok = lambda w: pl.BlockSpec((TOKEN_TILE, w), lambda i: (i, 0))
    return pl.pallas_call(
        _ffn1_kernel,
        grid=(rows // TOKEN_TILE,),
        in_specs=[tok(D_MODEL), _const_spec((1, D_MODEL)),
                  _const_spec((D_MODEL, D_FF)), _const_spec((D_MODEL, D_FF)), _const_spec((D_FF, D_MODEL)),
                  _const_spec((1, D_MODEL)), _const_spec((1, D_MODEL))],
        out_specs=[tok(D_MODEL), tok(D_MODEL)],
        out_shape=[jax.ShapeDtypeStruct((rows, D_MODEL), F32), jax.ShapeDtypeStruct((rows, D_MODEL), BF16)],
        compiler_params=pltpu.CompilerParams(
            dimension_semantics=("arbitrary",), vmem_limit_bytes=VMEM_LIMIT_BYTES),
        name="ffn1",
    )(x2d, g1, wg, wu, wo, p1, gm)


def _proj_kernel(h_ref, wm_ref, q1_ref, k1_ref, v1_ref, q4_ref, k4_ref, v4_ref,
                 q16_ref, k16_ref, v16_ref, u_ref, proj_scr, mod4_scr):
    @pl.when(pl.program_id(0) == 0)
    def _init():
        proj_scr[...] = jnp.zeros(proj_scr.shape, F32)

    lanes_per = ATTN_WIDTH // LANES
    q4_rows = TOKEN_TILE // 4
    outs = ((q1_ref, q4_ref, q16_ref), (k1_ref, k4_ref, k16_ref), (v1_ref, v4_ref, v16_ref))
    for t, (o1, o4, o16) in enumerate(outs):
        for c in range(lanes_per):
            cs = slice(c * LANES, (c + 1) * LANES)
            slab = t * lanes_per + c
            o1[:, cs] = proj_scr[slab].astype(BF16)
            for b in range(4):
                cls4 = proj_scr[slab, pl.ds(b, q4_rows, stride=4), :]
                o4[0, b, :, cs] = cls4.astype(BF16)
                mod4_scr[slab, b * q4_rows:(b + 1) * q4_rows, :] = cls4
            for b in range(4):
                for a in range(4):
                    o16[0, 4 * a + b, :, cs] = mod4_scr[
                        slab, pl.ds(b * q4_rows + a, q4_rows // 4, stride=4), :].astype(BF16)
    for c in range(lanes_per):
        u_ref[:, c * LANES:(c + 1) * LANES] = proj_scr[3 * lanes_per + c].astype(BF16)

    proj = jnp.dot(h_ref[...], wm_ref[...], preferred_element_type=F32)
    for cb in range(PROJ_SLABS):
        slab = proj[:, cb * LANES:(cb + 1) * LANES]
        proj_scr[cb] = slab * QK_SCALE_LOG2 if cb < lanes_per else slab


def _proj(h2, wm, batch, seq_len):
    tm = TOKEN_TILE
    n_tiles = batch * seq_len // tm
    tok_now, _ = _token_specs(n_tiles, seq_len // tm, 0)
    tok_prev, res_prev = _token_specs(n_tiles, seq_len // tm, 1)
    nat = jax.ShapeDtypeStruct((batch * seq_len, ATTN_WIDTH), BF16)
    r4 = jax.ShapeDtypeStruct((batch, 4, seq_len // 4, ATTN_WIDTH), BF16)
    r16 = jax.ShapeDtypeStruct((batch, 16, seq_len // 16, ATTN_WIDTH), BF16)
    return pl.pallas_call(
        _proj_kernel,
        grid=(n_tiles + 1,),
        in_specs=[tok_now(D_MODEL), _const_spec((D_MODEL, 4 * ATTN_WIDTH))],
        out_specs=[tok_prev(ATTN_WIDTH)] * 3 + [res_prev(4, ATTN_WIDTH)] * 3
        + [res_prev(16, ATTN_WIDTH)] * 3 + [tok_prev(SSM_WIDTH)],
        out_shape=[nat] * 3 + [r4] * 3 + [r16] * 3 + [nat],
        scratch_shapes=[pltpu.VMEM((PROJ_SLABS, tm, LANES), F32),
                        pltpu.VMEM((3 * ATTN_WIDTH // LANES, tm, LANES), F32)],
        compiler_params=pltpu.CompilerParams(
            dimension_semantics=("arbitrary",), vmem_limit_bytes=VMEM_LIMIT_BYTES),
        name="proj",
    )(h2, wm)


def _attn_kernel(q_ref, kc_ref, kp_ref, vc_ref, vp_ref, o_ref, st_ref, bias_ref, *, blocks_per_seq, dilation):
    @pl.when(pl.program_id(0) == 0)
    def _build_bias_tables():
        qi = lax.broadcasted_iota(jnp.int32, (QBLK, 2 * QBLK), 0)
        ci = lax.broadcasted_iota(jnp.int32, (QBLK, 2 * QBLK), 1)
        steps = QBLK + qi - ci
        dist = (steps * dilation).astype(F32)
        band = jnp.where(steps >= 0, jnp.where(steps <= WINDOW_STEPS, 1, 0), 0)
        band_first = jnp.where(ci >= QBLK, band, 0)
        for h in range(N_HEADS):
            bias = -(2.0 ** (-8.0 * (h + 1) / N_HEADS) * LOG2_E) * dist
            bias_ref[0, h] = jnp.where(band == 1, bias, MASK_VALUE)
            bias_ref[1, h] = jnp.where(band_first == 1, bias, MASK_VALUE)

    n_blocks = ATTN_TILE // QBLK
    lane = lax.broadcasted_iota(jnp.int32, (1, 2 * HEAD_DIM), 1)
    head_mask = ((lane < HEAD_DIM).astype(BF16), (lane >= HEAD_DIM).astype(BF16))
    ones_cols = tuple(jnp.broadcast_to(mk, (2 * QBLK, 2 * HEAD_DIM)) for mk in head_mask)
    lane_f = lax.broadcasted_iota(jnp.int32, (QBLK, 2 * HEAD_DIM), 1)
    low_half = lane_f < HEAD_DIM

    for j in range(n_blocks):
        rows = slice(j * QBLK, (j + 1) * QBLK)
        if blocks_per_seq >= n_blocks:
            tiles_per_seq = blocks_per_seq // n_blocks
            table = (lax.rem(pl.program_id(0), tiles_per_seq) == 0).astype(jnp.int32) if j == 0 else 0
        else:
            table = 1 if j % blocks_per_seq == 0 else 0
        m_tile = jnp.zeros((QBLK, 2 * HEAD_DIM), F32)
        l_tile = jnp.ones((QBLK, 2 * HEAD_DIM), F32)
        for hp in range(N_HEADS // 2):
            cs = slice(2 * HEAD_DIM * hp, 2 * HEAD_DIM * (hp + 1))
            q = q_ref[rows, cs]
            if j == 0:
                kk = jnp.concatenate([kp_ref[:, cs], kc_ref[0:QBLK, cs]], axis=0)
                vv = jnp.concatenate([vp_ref[:, cs], vc_ref[0:QBLK, cs]], axis=0)
            else:
                kk = kc_ref[(j - 1) * QBLK:(j + 1) * QBLK, cs]
                vv = vc_ref[(j - 1) * QBLK:(j + 1) * QBLK, cs]
            ps, ms, vas = [], [], []
            for e in range(2):
                s = lax.dot_general(q * head_mask[e], kk, (((1,), (1,)), ((), ())),
                                    preferred_element_type=F32)
                s = s + bias_ref[table, 2 * hp + e]
                m = jnp.max(s, axis=-1, keepdims=True)
                ps.append(jnp.exp2(s - m).astype(BF16))
                ms.append(m)
                vas.append(jnp.concatenate([vv * head_mask[e], ones_cols[e]], axis=1))
            pv = jnp.dot(jnp.concatenate(ps, axis=1), jnp.concatenate(vas, axis=0),
                         preferred_element_type=F32)
            o_ref[rows, cs] = pv[:, :2 * HEAD_DIM].astype(BF16)
            mine = (lane_f & (HEAD_DIM - 1)) == hp
            m_tile = jnp.where(mine, jnp.where(low_half, ms[0], ms[1]), m_tile)
            l_tile = jnp.where(mine, pv[:, 2 * HEAD_DIM:], l_tile)
        st_ref[rows, 0:2 * HEAD_DIM] = m_tile
        st_ref[rows, 2 * HEAD_DIM:] = l_tile


def _attention_branch(q, k, v, seq_len, dilation):
    rows = q.shape[0]
    ratio = ATTN_TILE // QBLK
    cur = pl.BlockSpec((ATTN_TILE, ATTN_WIDTH), lambda i: (i, 0))
    prev = pl.BlockSpec((QBLK, ATTN_WIDTH), lambda i: (jnp.maximum(i * ratio - 1, 0), 0))
    return pl.pallas_call(
        functools.partial(_attn_kernel, blocks_per_seq=seq_len // QBLK, dilation=dilation),
        grid=(rows // ATTN_TILE,),
        in_specs=[cur, cur, prev, cur, prev],
        out_specs=[cur, pl.BlockSpec((ATTN_TILE, STATS_WIDTH), lambda i: (i, 0))],
        out_shape=[jax.ShapeDtypeStruct((rows, ATTN_WIDTH), BF16),
                   jax.ShapeDtypeStruct((rows, STATS_WIDTH), F32)],
        scratch_shapes=[pltpu.VMEM((2, N_HEADS, QBLK, 2 * QBLK), F32)],
        compiler_params=pltpu.CompilerParams(dimension_semantics=("arbitrary",)),
        name=f"attn_s{seq_len}",
    )(q, k, k, v, v)


def _cexp(n, lam_re, lam_im):
    mag = jnp.exp(n * lam_re)
    return mag * jnp.cos(n * lam_im), mag * jnp.sin(n * lam_im)


def _ssm_kernel(u_ref, colp_ref, rowp_ref, dcol_ref, y_ref, a1_scr, cw_scr, *, chunks_per_seq):
    for gi in range(SSM_GROUPS_PER_STEP):
        rows = pl.ds(gi * SSM_GROUP, SSM_GROUP)
        _ssm_group(u_ref.at[rows], colp_ref.at[gi], rowp_ref.at[gi], dcol_ref.at[gi], y_ref.at[rows],
                   a1_scr.at[gi], cw_scr.at[gi], chunks_per_seq)


def _ssm_group(u_ref, colp_ref, rowp_ref, dcol_ref, y_ref, a1_scr, cw_scr, chunks_per_seq):
    T, P, C = SSM_CHUNK, STATE_DIM, SSM_GROUP
    n_rows = C * T
    n_cols = u_ref.shape[-1]
    lane = lax.broadcasted_iota(jnp.int32, (1, LANES), 1)
    low = lane < P

    a_re = colp_ref[:, 0:1]
    a_im = colp_ref[:, 1:2]
    dt = jnp.exp(colp_ref[:, 34:35])
    lam_re, lam_im = dt * a_re, dt * a_im
    ab_re, ab_im = _cexp(1.0, lam_re, lam_im)
    inv_a2 = 1.0 / (a_re * a_re + a_im * a_im)
    nr, ni = ab_re - 1.0, ab_im
    cf_re = (nr * a_re + ni * a_im) * inv_a2
    cf_im = (ni * a_re - nr * a_im) * inv_a2

    rev = (T - 1 - (lane & (T - 1))).astype(F32)
    pw_re, pw_im = _cexp(rev, lam_re, lam_im)
    g_re, g_im = [], []
    for k in range(C // 2):
        b_re = jnp.where(low, colp_ref[:, 2 + 2 * k:3 + 2 * k], colp_ref[:, 3 + 2 * k:4 + 2 * k])
        b_im = jnp.where(low, colp_ref[:, 18 + 2 * k:19 + 2 * k], colp_ref[:, 19 + 2 * k:20 + 2 * k])
        bb_re = cf_re * b_re - cf_im * b_im
        bb_im = cf_re * b_im + cf_im * b_re
        g_re.append(pw_re * bb_re - pw_im * bb_im)
        g_im.append(pw_re * bb_im + pw_im * bb_re)
    gm = jnp.concatenate([jnp.concatenate(g_re, axis=1), jnp.concatenate(g_im, axis=1)], axis=0)
    a1_scr[n_rows:, :] = gm.astype(BF16)

    c_re2 = rowp_ref[0:C, :]
    c_im2 = rowp_ref[C:2 * C, :]
    kr = jnp.dot(jnp.where(low, c_re2, -c_im2), gm, precision=lax.Precision.HIGHEST,
                 preferred_element_type=F32)
    krow = lax.broadcasted_iota(jnp.int32, (C, C * T), 0)
    klane = lax.broadcasted_iota(jnp.int32, (C, C * T), 1)
    kr = kr + jnp.where(klane == krow * T + (T - 1), dcol_ref[:, 0:1], 0.0)

    t_idx = lax.broadcasted_iota(jnp.int32, (T, LANES), 0)
    s_idx = lax.broadcasted_iota(jnp.int32, (T, LANES), 1) & (T - 1)
    causal = s_idx <= t_idx
    for c in range(C):
        for k in range(C * T // LANES):
            z = jnp.broadcast_to(kr[c:c + 1, k * LANES:(k + 1) * LANES], (T, LANES))
            z = pltpu.roll(z, LANES - (T - 1), axis=1, stride=1, stride_axis=0)
            a1_scr[c * T:(c + 1) * T, k * LANES:(k + 1) * LANES] = jnp.where(causal, z, 0.0).astype(BF16)

    dt_r = jnp.exp(rowp_ref[34:35, :])
    steps = (lax.broadcasted_iota(jnp.int32, (T, LANES), 0) + 1).astype(F32)
    pr, pi = _cexp(steps, dt_r * rowp_ref[32:33, :], dt_r * rowp_ref[33:34, :])
    for c in range(C):
        cr, ci = c_re2[c:c + 1, :], c_im2[c:c + 1, :]
        cw_scr[c * T:(c + 1) * T, :] = jnp.where(low, cr * pr - ci * pi, -(cr * pi + ci * pr)).astype(BF16)

    u = u_ref[...].reshape(n_rows, n_cols)
    y1 = jnp.dot(a1_scr[...], u, preferred_element_type=F32)
    xr = y1[n_rows:n_rows + P]
    xi = y1[n_rows + P:]
    kidx = lax.broadcasted_iota(jnp.int32, (P, n_cols), 1) & (chunks_per_seq - 1)
    mr, mi = _cexp(float(T), lam_re, lam_im)
    sh = 1
    while sh < chunks_per_seq:
        rr = pltpu.roll(xr, sh, axis=1)
        ri = pltpu.roll(xi, sh, axis=1)
        ok = kidx >= sh
        xr, xi = (xr + jnp.where(ok, mr * rr - mi * ri, 0.0),
                  xi + jnp.where(ok, mr * ri + mi * rr, 0.0))
        mr, mi = mr * mr - mi * mi, 2.0 * mr * mi
        sh *= 2
    ok = kidx >= 1
    hr = jnp.where(ok, pltpu.roll(xr, 1, axis=1), 0.0)
    hi = jnp.where(ok, pltpu.roll(xi, 1, axis=1), 0.0)
    h = jnp.concatenate([hr, hi], axis=0).astype(BF16)
    y = y1[:n_rows] + jnp.dot(cw_scr[...], h, preferred_element_type=F32)
    y_ref[...] = jax.nn.gelu(y).astype(BF16).reshape(C, T, n_cols)


def _ssm(ut, colp, rowp, dcol, chunks_per_seq):
    n_cols = ut.shape[-1]
    n_rows = SSM_GROUP * SSM_CHUNK
    per_step = SSM_GROUPS_PER_STEP
    blk = pl.BlockSpec((per_step * SSM_GROUP, SSM_CHUNK, n_cols), lambda g: (g, 0, 0))
    per_group = lambda a: pl.BlockSpec((per_step,) + a.shape[1:], lambda g: (g, 0, 0))
    return pl.pallas_call(
        functools.partial(_ssm_kernel, chunks_per_seq=chunks_per_seq),
        grid=(N_SSM_GROUPS // per_step,),
        in_specs=[blk, per_group(colp), per_group(rowp), per_group(dcol)],
        out_specs=blk,
        out_shape=jax.ShapeDtypeStruct(ut.shape, BF16),
        scratch_shapes=[pltpu.VMEM((per_step, n_rows + 2 * STATE_DIM, n_rows), BF16),
                        pltpu.VMEM((per_step, n_rows, 2 * STATE_DIM), BF16)],
        compiler_params=pltpu.CompilerParams(dimension_semantics=("arbitrary",)),
        name="ssm",
    )(ut, colp, rowp, dcol)


def _pack_ssm_params(a_re, a_im, log_dt, b_re, b_im, c_re, c_im, d_skip):
    G, P, C = N_SSM_GROUPS, STATE_DIM, SSM_GROUP
    f = lambda t: t.astype(F32)
    log_dt_col = jnp.broadcast_to(f(log_dt)[:, None, None], (G, P, 1))
    colp = jnp.concatenate([f(a_re)[:, :, None], f(a_im)[:, :, None], f(b_re), f(b_im), log_dt_col,
                            jnp.zeros((G, P, LANES - 3 - 2 * C), F32)], axis=2)
    twice = lambda t: jnp.concatenate([f(t), f(t)], axis=-1)
    rowp = jnp.concatenate([twice(c_re), twice(c_im), twice(a_re)[:, None, :], twice(a_im)[:, None, :],
                            jnp.broadcast_to(f(log_dt)[:, None, None], (G, 1, 2 * P)),
                            jnp.zeros((G, 5, 2 * P), F32)], axis=1)
    dcol = jnp.broadcast_to(f(d_skip).reshape(G, C, 1), (G, C, LANES))
    return colp, rowp, dcol


def _mix_ffn2_kernel(x1_ref, o1_ref, o4_ref, o16_ref, l1_ref, l4_ref, l16_ref, z_ref,
                     wglu_ref, bglu_ref, wout_ref, gpost_ref, g3_ref, wg_ref, wu_ref, wo_ref, p3_ref,
                     y_ref, o_scr, l_scr, x2_scr):
    tm = TOKEN_TILE
    step = pl.program_id(0)

    @pl.when(step == 0)
    def _init():
        x2_scr[...] = jnp.zeros(x2_scr.shape, F32)

    prev_slot = lax.rem(step + 1, 2)
    h3 = _rms(x2_scr[prev_slot], g3_ref[...]).astype(BF16)
    y_ref[...] = _swiglu_ffn(lambda: x2_scr[prev_slot], h3, wg_ref, wu_ref, wo_ref, p3_ref[...])

    for n, (d, o_ref, l_ref) in enumerate(((4, o4_ref, l4_ref), (16, o16_ref, l16_ref))):
        for r in range(d):
            for c in range(ATTN_WIDTH // LANES):
                o_scr[n, c, pl.ds(r, tm // d, stride=d), :] = o_ref[0, r, :, c * LANES:(c + 1) * LANES].astype(F32)
            for c in range(STATS_WIDTH // LANES):
                l_scr[n, c, pl.ds(r, tm // d, stride=d), :] = l_ref[0, r, :, c * LANES:(c + 1) * LANES]
    ms = (l1_ref[:, 0:LANES], l_scr[0, 0], l_scr[1, 0])
    ls = (l1_ref[:, LANES:], l_scr[0, 1], l_scr[1, 1])
    m = jnp.maximum(jnp.maximum(ms[0], ms[1]), ms[2])
    es = [jnp.exp2(mn - m) for mn in ms]
    inv = 1.0 / (es[0] * ls[0] + es[1] * ls[1] + es[2] * ls[2])
    ws = [e * inv for e in es]
    low_half = lax.broadcasted_iota(jnp.int32, (tm, 2 * HEAD_DIM), 1) < HEAD_DIM
    pairs = []
    for hp in range(N_HEADS // 2):
        cs = slice(2 * HEAD_DIM * hp, 2 * HEAD_DIM * (hp + 1))
        os_ = (o1_ref[:, cs].astype(F32), o_scr[0, hp], o_scr[1, hp])
        acc = jnp.zeros((tm, 2 * HEAD_DIM), F32)
        for w, o in zip(ws, os_):
            wexp = jnp.where(low_half, w[:, hp:hp + 1], w[:, HEAD_DIM + hp:HEAD_DIM + hp + 1])
            acc = acc + wexp * o
        pairs.append(acc)
    attn = jnp.concatenate(pairs, axis=-1).astype(BF16)
    z = z_ref[...]
    gate = jax.nn.sigmoid(jnp.dot(z, wglu_ref[...], preferred_element_type=F32) + bglu_ref[...])
    ssm = (z.astype(F32) * gate).astype(BF16)
    mixed = (jnp.dot(attn, wout_ref[0:ATTN_WIDTH, :], preferred_element_type=F32)
             + jnp.dot(ssm, wout_ref[ATTN_WIDTH:, :], preferred_element_type=F32))
    x2_scr[lax.rem(step, 2)] = x1_ref[...] + _rms(mixed, gpost_ref[...])


def _mix_ffn2(x1, o1, o4, o16, l1, l4, l16, z, wglu, bglu, wout, gpost, g3, wg, wu, wo, p3, seq_len):
    rows = x1.shape[0]
    tm = TOKEN_TILE
    n_tiles = rows // tm
    tok_now, res_now = _token_specs(n_tiles, seq_len // tm, 0)
    tok_prev, _ = _token_specs(n_tiles, seq_len // tm, 1)
    return pl.pallas_call(
        _mix_ffn2_kernel,
        grid=(n_tiles + 1,),
        in_specs=[tok_now(D_MODEL), tok_now(ATTN_WIDTH), res_now(4, ATTN_WIDTH), res_now(16, ATTN_WIDTH),
                  tok_now(STATS_WIDTH), res_now(4, STATS_WIDTH), res_now(16, STATS_WIDTH), tok_now(SSM_WIDTH),
                  _const_spec((SSM_WIDTH, SSM_WIDTH)), _const_spec((1, SSM_WIDTH)),
                  _const_spec((D_MODEL, D_MODEL)), _const_spec((1, D_MODEL)), _const_spec((1, D_MODEL)),
                  _const_spec((D_MODEL, D_FF)), _const_spec((D_MODEL, D_FF)), _const_spec((D_FF, D_MODEL)),
                  _const_spec((1, D_MODEL))],
        out_specs=tok_prev(D_MODEL),
        out_shape=jax.ShapeDtypeStruct((rows, D_MODEL), F32),
        scratch_shapes=[pltpu.VMEM((2, ATTN_WIDTH // LANES, tm, LANES), F32),
                        pltpu.VMEM((2, STATS_WIDTH // LANES, tm, LANES), F32),
                        pltpu.VMEM((2, tm, D_MODEL), F32)],
        compiler_params=pltpu.CompilerParams(
            dimension_semantics=("arbitrary",), vmem_limit_bytes=VMEM_LIMIT_BYTES),
        name="mix_ffn2",
    )(x1, o1, o4, o16, l1, l4, l16, z, wglu, bglu, wout, gpost, g3, wg, wu, wo, p3)


def _split_ffn_weights(w_in, w_out):
    return w_in[:, :D_FF].astype(BF16), w_in[:, D_FF:].astype(BF16), w_out.astype(BF16)


def _row(v):
    return v.astype(F32).reshape(1, -1)


def kernel(x, ffn1_pre_g, ffn1_w_in, ffn1_w_out, ffn1_post_g, mix_pre_g, w_mix_in, a_re, a_im, log_dt, b_re, b_im, c_re, c_im, d_skip, w_glu, b_glu, w_mix_out, mix_post_g, ffn2_pre_g, ffn2_w_in, ffn2_w_out, ffn2_post_g):
    B, S, _ = x.shape
    depth = ffn1_pre_g.shape[0]
    n_chunks = S // SSM_CHUNK
    for l in range(depth):
        wg1, wu1, wo1 = _split_ffn_weights(ffn1_w_in[l], ffn1_w_out[l])
        wg2, wu2, wo2 = _split_ffn_weights(ffn2_w_in[l], ffn2_w_out[l])
        x1, h2 = _ffn1(x.reshape(B * S, D_MODEL), _row(ffn1_pre_g[l]), wg1, wu1, wo1, _row(ffn1_post_g[l]),
                       _row(mix_pre_g[l]))
        q1, k1, v1, q4, k4, v4, q16, k16, v16, u = _proj(h2, w_mix_in[l].astype(BF16), B, S)

        outs = []
        for d, (q, k, v) in zip(DILATIONS, ((q1, k1, v1), (q4, k4, v4), (q16, k16, v16))):
            flat = lambda t: t.reshape(B * S, ATTN_WIDTH)
            outs.append(_attention_branch(flat(q), flat(k), flat(v), S // d, d))
        (o1, l1), (o4, l4), (o16, l16) = outs
        o4 = o4.reshape(B, 4, S // 4, ATTN_WIDTH)
        l4 = l4.reshape(B, 4, S // 4, STATS_WIDTH)
        o16 = o16.reshape(B, 16, S // 16, ATTN_WIDTH)
        l16 = l16.reshape(B, 16, S // 16, STATS_WIDTH)

        colp, rowp, dcol = _pack_ssm_params(a_re[l], a_im[l], log_dt[l], b_re[l], b_im[l], c_re[l], c_im[l],
                                            d_skip[l])
        ut = u.reshape(B, n_chunks, SSM_CHUNK, SSM_WIDTH).transpose(3, 2, 0, 1)
        ut = ut.reshape(SSM_WIDTH, SSM_CHUNK, B * n_chunks)
        zt = _ssm(ut, colp, rowp, dcol, n_chunks)
        z = zt.reshape(SSM_WIDTH, SSM_CHUNK, B, n_chunks).transpose(2, 3, 1, 0).reshape(B * S, SSM_WIDTH)

        x = _mix_ffn2(x1, o1, o4, o16, l1, l4, l16, z, w_glu[l].astype(BF16), _row(b_glu[l]),
                      w_mix_out[l].astype(BF16), _row(mix_post_g[l]), _row(ffn2_pre_g[l]), wg2, wu2, wo2,
                      _row(ffn2_post_g[l]), S).reshape(B, S, D_MODEL)
    return x
```

```python
import functools

import jax
import jax.numpy as jnp
from jax import lax
from jax.experimental import pallas as pl
from jax.experimental.pallas import tpu as pltpu

F32 = jnp.float32
BF16 = jnp.bfloat16

D_MODEL = 1024
ATTN_WIDTH = 512
SSM_WIDTH = 512
HEAD_DIM = 64
N_HEADS = 8
DILATIONS = (1, 4, 16)
WINDOW_STEPS = 128
QBLK = 128
SSM_GROUP = 16
N_SSM_GROUPS = 32
STATE_DIM = 64
D_FF = 2816
NORM_EPS = 1e-6

TOKEN_TILE = 512
FF_CHUNKS = (256,) * 11
SSM_CHUNK = 64
SSM_GROUPS_PER_STEP = 2
ATTN_TILE = 4096
MASK_VALUE = float("-inf")
LANES = 128
LOG2_E = 1.4426950408889634
QK_SCALE_LOG2 = HEAD_DIM ** -0.5 * LOG2_E
STATS_WIDTH = 2 * HEAD_DIM
L_LANE_OFFSET = 16
PROJ_SLABS = 4 * ATTN_WIDTH // LANES
VMEM_LIMIT_BYTES = 56 * 1024 * 1024


def _rms(x, g):
    return x * lax.rsqrt(jnp.mean(x * x, axis=-1, keepdims=True) + NORM_EPS) * g


def _swiglu_chunks(h, acc, chunks, wg_ref, wu_ref, wo_ref):
    for start, width in chunks:
        sl = slice(start, start + width)
        gate = jnp.dot(h, wg_ref[:, sl], preferred_element_type=F32)
        up = jnp.dot(h, wu_ref[:, sl], preferred_element_type=F32)
        act = (gate * jax.nn.sigmoid(gate) * up).astype(BF16)
        acc = acc + jnp.dot(act, wo_ref[sl, :], preferred_element_type=F32)
    return acc


def _ff_chunks():
    starts = [sum(FF_CHUNKS[:j]) for j in range(len(FF_CHUNKS))]
    return list(zip(starts, FF_CHUNKS))


def _swiglu_ffn(read_x, h, wg_ref, wu_ref, wo_ref, post_g):
    acc = _swiglu_chunks(h, jnp.zeros((h.shape[0], D_MODEL), F32), _ff_chunks(), wg_ref, wu_ref, wo_ref)
    return read_x() + 0.5 * _rms(acc, post_g)


def _const_spec(shape):
    return pl.BlockSpec(shape, lambda *_: (0,) * len(shape), pipeline_mode=pl.Buffered(1))


def _token_specs(n_tiles, tiles_per_seq, lag):
    tm = TOKEN_TILE

    def tile(i):
        return jnp.clip(i - lag, 0, n_tiles - 1)

    def tok(w):
        return pl.BlockSpec((tm, w), lambda i: (tile(i), 0))

    def res(d, w):
        return pl.BlockSpec((1, d, tm // d, w),
                            lambda i: (tile(i) // tiles_per_seq, 0, tile(i) % tiles_per_seq, 0))

    return tok, res


def _ffn1_proj_kernel(x_ref, g1_ref, wg_ref, wu_ref, wo_ref, p1_ref, gm_ref, wm_ref,
                      x1_ref, q1_ref, k1_ref, v1_ref, q4_ref, k4_ref, v4_ref,
                      q16_ref, k16_ref, v16_ref, u_ref, proj_scr, mod4_scr):
    @pl.when(pl.program_id(0) == 0)
    def _init():
        proj_scr[...] = jnp.zeros(proj_scr.shape, F32)

    lanes_per = ATTN_WIDTH // LANES
    q4_rows = TOKEN_TILE // 4
    outs = ((q1_ref, q4_ref, q16_ref), (k1_ref, k4_ref, k16_ref), (v1_ref, v4_ref, v16_ref))
    for t, (o1, o4, o16) in enumerate(outs):
        for c in range(lanes_per):
            cs = slice(c * LANES, (c + 1) * LANES)
            slab = t * lanes_per + c
            o1[:, cs] = proj_scr[slab].astype(BF16)
            for b in range(4):
                cls4 = proj_scr[slab, pl.ds(b, q4_rows, stride=4), :]
                o4[0, b, :, cs] = cls4.astype(BF16)
                mod4_scr[slab, b * q4_rows:(b + 1) * q4_rows, :] = cls4
            for b in range(4):
                for a in range(4):
                    o16[0, 4 * a + b, :, cs] = mod4_scr[
                        slab, pl.ds(b * q4_rows + a, q4_rows // 4, stride=4), :].astype(BF16)
    for c in range(lanes_per):
        u_ref[:, c * LANES:(c + 1) * LANES] = proj_scr[3 * lanes_per + c].astype(BF16)

    h1 = _rms(x_ref[...], g1_ref[...]).astype(BF16)
    x1 = _swiglu_ffn(lambda: x_ref[...], h1, wg_ref, wu_ref, wo_ref, p1_ref[...])
    x1_ref[...] = x1
    h = _rms(x1, gm_ref[...]).astype(BF16)
    proj = jnp.dot(h, wm_ref[...], preferred_element_type=F32)
    for cb in range(PROJ_SLABS):
        slab = proj[:, cb * LANES:(cb + 1) * LANES]
        proj_scr[cb] = slab * QK_SCALE_LOG2 if cb < lanes_per else slab


def _ffn1_proj(x, g1, wg, wu, wo, p1, gm, wm):
    B, S, _ = x.shape
    tm = TOKEN_TILE
    n_tiles = B * S // tm
    tok_now, _ = _token_specs(n_tiles, S // tm, 0)
    tok_prev, res_prev = _token_specs(n_tiles, S // tm, 1)
    nat = jax.ShapeDtypeStruct((B * S, ATTN_WIDTH), BF16)
    r4 = jax.ShapeDtypeStruct((B, 4, S // 4, ATTN_WIDTH), BF16)
    r16 = jax.ShapeDtypeStruct((B, 16, S // 16, ATTN_WIDTH), BF16)
    return pl.pallas_call(
        _ffn1_proj_kernel,
        grid=(n_tiles + 1,),
        in_specs=[tok_now(D_MODEL), _const_spec((1, D_MODEL)),
                  _const_spec((D_MODEL, D_FF)), _const_spec((D_MODEL, D_FF)), _const_spec((D_FF, D_MODEL)),
                  _const_spec((1, D_MODEL)), _const_spec((1, D_MODEL)), _const_spec((D_MODEL, 4 * ATTN_WIDTH))],
        out_specs=[tok_now(D_MODEL)] + [tok_prev(ATTN_WIDTH)] * 3 + [res_prev(4, ATTN_WIDTH)] * 3
        + [res_prev(16, ATTN_WIDTH)] * 3 + [tok_prev(SSM_WIDTH)],
        out_shape=[jax.ShapeDtypeStruct((B * S, D_MODEL), F32)] + [nat] * 3 + [r4] * 3 + [r16] * 3 + [nat],
        scratch_shapes=[pltpu.VMEM((PROJ_SLABS, tm, LANES), F32),
                        pltpu.VMEM((3 * ATTN_WIDTH // LANES, tm, LANES), F32)],
        compiler_params=pltpu.CompilerParams(
            dimension_semantics=("arbitrary",), vmem_limit_bytes=VMEM_LIMIT_BYTES),
        name="ffn1_proj",
    )(x.reshape(B * S, D_MODEL), g1, wg, wu, wo, p1, gm, wm)


def _attn_kernel(q_ref, kc_ref, kp_ref, vc_ref, vp_ref, o_ref, st_ref, bias_ref, *, blocks_per_seq, dilation):
    @pl.when(pl.program_id(0) == 0)
    def _build_bias_tables():
        qi = lax.broadcasted_iota(jnp.int32, (QBLK, 2 * QBLK), 0)
        ci = lax.broadcasted_iota(jnp.int32, (QBLK, 2 * QBLK), 1)
        steps = QBLK + qi - ci
        dist = (steps * dilation).astype(F32)
        band = jnp.where(steps >= 0, jnp.where(steps <= WINDOW_STEPS, 1, 0), 0)
        band_first = jnp.where(ci >= QBLK, band, 0)
        for h in range(N_HEADS):
            bias = -(2.0 ** (-8.0 * (h + 1) / N_HEADS) * LOG2_E) * dist
            bias_ref[0, h] = jnp.where(band == 1, bias, MASK_VALUE)
            bias_ref[1, h] = jnp.where(band_first == 1, bias, MASK_VALUE)

    n_blocks = ATTN_TILE // QBLK
    lane = lax.broadcasted_iota(jnp.int32, (1, 2 * HEAD_DIM), 1)
    head_mask = ((lane < HEAD_DIM).astype(BF16), (lane >= HEAD_DIM).astype(BF16))
    ones_cols = tuple(jnp.broadcast_to(mk, (2 * QBLK, 2 * HEAD_DIM)) for mk in head_mask)
    lane_f = lax.broadcasted_iota(jnp.int32, (QBLK, 2 * HEAD_DIM), 1)
    low_half = lane_f < HEAD_DIM

    for j in range(n_blocks):
        rows = slice(j * QBLK, (j + 1) * QBLK)
        if blocks_per_seq >= n_blocks:
            tiles_per_seq = blocks_per_seq // n_blocks
            table = (lax.rem(pl.program_id(0), tiles_per_seq) == 0).astype(jnp.int32) if j == 0 else 0
        else:
            table = 1 if j % blocks_per_seq == 0 else 0
        st_tile = jnp.ones((QBLK, 2 * HEAD_DIM), F32)
        for hp in range(N_HEADS // 2):
            cs = slice(2 * HEAD_DIM * hp, 2 * HEAD_DIM * (hp + 1))
            q = q_ref[rows, cs]
            if j == 0:
                kk = jnp.concatenate([kp_ref[:, cs], kc_ref[0:QBLK, cs]], axis=0)
                vv = jnp.concatenate([vp_ref[:, cs], vc_ref[0:QBLK, cs]], axis=0)
            else:
                kk = kc_ref[(j - 1) * QBLK:(j + 1) * QBLK, cs]
                vv = vc_ref[(j - 1) * QBLK:(j + 1) * QBLK, cs]
            ps, ms, vas = [], [], []
            for e in range(2):
                s = lax.dot_general(q * head_mask[e], kk, (((1,), (1,)), ((), ())),
                                    preferred_element_type=F32)
                s = s + bias_ref[table, 2 * hp + e]
                m = jnp.max(s, axis=-1, keepdims=True)
                ps.append(jnp.exp2(s - m).astype(BF16))
                ms.append(m)
                vas.append(jnp.concatenate([vv * head_mask[e], ones_cols[e]], axis=1))
            pv = jnp.dot(jnp.concatenate(ps, axis=1), jnp.concatenate(vas, axis=0),
                         preferred_element_type=F32)
            o_ref[rows, cs] = pv[:, :2 * HEAD_DIM].astype(BF16)
            in_half = lane_f & (HEAD_DIM - 1)
            st_tile = jnp.where(in_half == hp, jnp.where(low_half, ms[0], ms[1]),
                                jnp.where(in_half == hp + L_LANE_OFFSET, pv[:, 2 * HEAD_DIM:], st_tile))
        st_ref[rows, :] = st_tile


def _attention_branch(q, k, v, seq_len, dilation):
    rows = q.shape[0]
    ratio = ATTN_TILE // QBLK
    cur = pl.BlockSpec((ATTN_TILE, ATTN_WIDTH), lambda i: (i, 0))
    prev = pl.BlockSpec((QBLK, ATTN_WIDTH), lambda i: (jnp.maximum(i * ratio - 1, 0), 0))
    return pl.pallas_call(
        functools.partial(_attn_kernel, blocks_per_seq=seq_len // QBLK, dilation=dilation),
        grid=(rows // ATTN_TILE,),
        in_specs=[cur, cur, prev, cur, prev],
        out_specs=[cur, pl.BlockSpec((ATTN_TILE, STATS_WIDTH), lambda i: (i, 0))],
        out_shape=[jax.ShapeDtypeStruct((rows, ATTN_WIDTH), BF16),
                   jax.ShapeDtypeStruct((rows, STATS_WIDTH), F32)],
        scratch_shapes=[pltpu.VMEM((2, N_HEADS, QBLK, 2 * QBLK), F32)],
        compiler_params=pltpu.CompilerParams(dimension_semantics=("arbitrary",)),
        name=f"attn_s{seq_len}",
    )(q, k, k, v, v)


def _cexp(n, lam_re, lam_im):
    mag = jnp.exp(n * lam_re)
    return mag * jnp.cos(n * lam_im), mag * jnp.sin(n * lam_im)


def _ssm_kernel(u_ref, colp_ref, rowp_ref, dcol_ref, y_ref, a1_scr, cw_scr, *, chunks_per_seq):
    for gi in range(SSM_GROUPS_PER_STEP):
        rows = pl.ds(gi * SSM_GROUP, SSM_GROUP)
        _ssm_group(u_ref.at[rows], colp_ref.at[gi], rowp_ref.at[gi], dcol_ref.at[gi], y_ref.at[rows],
                   a1_scr.at[gi], cw_scr.at[gi], chunks_per_seq)


def _ssm_group(u_ref, colp_ref, rowp_ref, dcol_ref, y_ref, a1_scr, cw_scr, chunks_per_seq):
    T, P, C = SSM_CHUNK, STATE_DIM, SSM_GROUP
    n_rows = C * T
    n_cols = u_ref.shape[-1]
    lane = lax.broadcasted_iota(jnp.int32, (1, LANES), 1)
    low = lane < P

    a_re = colp_ref[:, 0:1]
    a_im = colp_ref[:, 1:2]
    dt = jnp.exp(colp_ref[:, 34:35])
    lam_re, lam_im = dt * a_re, dt * a_im
    ab_re, ab_im = _cexp(1.0, lam_re, lam_im)
    inv_a2 = 1.0 / (a_re * a_re + a_im * a_im)
    nr, ni = ab_re - 1.0, ab_im
    cf_re = (nr * a_re + ni * a_im) * inv_a2
    cf_im = (ni * a_re - nr * a_im) * inv_a2

    rev = (T - 1 - (lane & (T - 1))).astype(F32)
    pw_re, pw_im = _cexp(rev, lam_re, lam_im)
    g_re, g_im = [], []
    for k in range(C // 2):
        b_re = jnp.where(low, colp_ref[:, 2 + 2 * k:3 + 2 * k], colp_ref[:, 3 + 2 * k:4 + 2 * k])
        b_im = jnp.where(low, colp_ref[:, 18 + 2 * k:19 + 2 * k], colp_ref[:, 19 + 2 * k:20 + 2 * k])
        bb_re = cf_re * b_re - cf_im * b_im
        bb_im = cf_re * b_im + cf_im * b_re
        g_re.append(pw_re * bb_re - pw_im * bb_im)
        g_im.append(pw_re * bb_im + pw_im * bb_re)
    gm = jnp.concatenate([jnp.concatenate(g_re, axis=1), jnp.concatenate(g_im, axis=1)], axis=0)
    a1_scr[n_rows:, :] = gm.astype(BF16)

    c_re2 = rowp_ref[0:C, :]
    c_im2 = rowp_ref[C:2 * C, :]
    kr = jnp.dot(jnp.where(low, c_re2, -c_im2), gm, precision=lax.Precision.HIGHEST,
                 preferred_element_type=F32)
    krow = lax.broadcasted_iota(jnp.int32, (C, C * T), 0)
    klane = lax.broadcasted_iota(jnp.int32, (C, C * T), 1)
    kr = kr + jnp.where(klane == krow * T + (T - 1), dcol_ref[:, 0:1], 0.0)

    t_idx = lax.broadcasted_iota(jnp.int32, (T, LANES), 0)
    s_idx = lax.broadcasted_iota(jnp.int32, (T, LANES), 1) & (T - 1)
    causal = s_idx <= t_idx
    for c in range(C):
        for k in range(C * T // LANES):
            z = jnp.broadcast_to(kr[c:c + 1, k * LANES:(k + 1) * LANES], (T, LANES))
            z = pltpu.roll(z, LANES - (T - 1), axis=1, stride=1, stride_axis=0)
            a1_scr[c * T:(c + 1) * T, k * LANES:(k + 1) * LANES] = jnp.where(causal, z, 0.0).astype(BF16)

    dt_r = jnp.exp(rowp_ref[34:35, :])
    steps = (lax.broadcasted_iota(jnp.int32, (T, LANES), 0) + 1).astype(F32)
    pr, pi = _cexp(steps, dt_r * rowp_ref[32:33, :], dt_r * rowp_ref[33:34, :])
    for c in range(C):
        cr, ci = c_re2[c:c + 1, :], c_im2[c:c + 1, :]
        cw_scr[c * T:(c + 1) * T, :] = jnp.where(low, cr * pr - ci * pi, -(cr * pi + ci * pr)).astype(BF16)

    u = u_ref[...].reshape(n_rows, n_cols)
    y1 = jnp.dot(a1_scr[...], u, preferred_element_type=F32)
    xr = y1[n_rows:n_rows + P]
    xi = y1[n_rows + P:]
    kidx = lax.broadcasted_iota(jnp.int32, (P, n_cols), 1) & (chunks_per_seq - 1)
    mr, mi = _cexp(float(T), lam_re, lam_im)
    sh = 1
    while sh < chunks_per_seq:
        rr = pltpu.roll(xr, sh, axis=1)
        ri = pltpu.roll(xi, sh, axis=1)
        ok = kidx >= sh
        xr, xi = (xr + jnp.where(ok, mr * rr - mi * ri, 0.0),
                  xi + jnp.where(ok, mr * ri + mi * rr, 0.0))
        mr, mi = mr * mr - mi * mi, 2.0 * mr * mi
        sh *= 2
    ok = kidx >= 1
    hr = jnp.where(ok, pltpu.roll(xr, 1, axis=1), 0.0)
    hi = jnp.where(ok, pltpu.roll(xi, 1, axis=1), 0.0)
    h = jnp.concatenate([hr, hi], axis=0).astype(BF16)
    y = y1[:n_rows] + jnp.dot(cw_scr[...], h, preferred_element_type=F32)
    y_ref[...] = jax.nn.gelu(y).astype(BF16).reshape(C, T, n_cols)


def _ssm(ut, colp, rowp, dcol, chunks_per_seq):
    n_cols = ut.shape[-1]
    n_rows = SSM_GROUP * SSM_CHUNK
    per_step = SSM_GROUPS_PER_STEP
    blk = pl.BlockSpec((per_step * SSM_GROUP, SSM_CHUNK, n_cols), lambda g: (g, 0, 0))
    per_group = lambda a: pl.BlockSpec((per_step,) + a.shape[1:], lambda g: (g, 0, 0))
    return pl.pallas_call(
        functools.partial(_ssm_kernel, chunks_per_seq=chunks_per_seq),
        grid=(N_SSM_GROUPS // per_step,),
        in_specs=[blk, per_group(colp), per_group(rowp), per_group(dcol)],
        out_specs=blk,
        out_shape=jax.ShapeDtypeStruct(ut.shape, BF16),
        scratch_shapes=[pltpu.VMEM((per_step, n_rows + 2 * STATE_DIM, n_rows), BF16),
                        pltpu.VMEM((per_step, n_rows, 2 * STATE_DIM), BF16)],
        compiler_params=pltpu.CompilerParams(dimension_semantics=("arbitrary",)),
        name="ssm",
    )(ut, colp, rowp, dcol)


def _pack_ssm_params(a_re, a_im, log_dt, b_re, b_im, c_re, c_im, d_skip):
    G, P, C = N_SSM_GROUPS, STATE_DIM, SSM_GROUP
    f = lambda t: t.astype(F32)
    log_dt_col = jnp.broadcast_to(f(log_dt)[:, None, None], (G, P, 1))
    colp = jnp.concatenate([f(a_re)[:, :, None], f(a_im)[:, :, None], f(b_re), f(b_im), log_dt_col,
                            jnp.zeros((G, P, LANES - 3 - 2 * C), F32)], axis=2)
    twice = lambda t: jnp.concatenate([f(t), f(t)], axis=-1)
    rowp = jnp.concatenate([twice(c_re), twice(c_im), twice(a_re)[:, None, :], twice(a_im)[:, None, :],
                            jnp.broadcast_to(f(log_dt)[:, None, None], (G, 1, 2 * P)),
                            jnp.zeros((G, 5, 2 * P), F32)], axis=1)
    dcol = jnp.broadcast_to(f(d_skip).reshape(G, C, 1), (G, C, LANES))
    return colp, rowp, dcol


def _mix_ffn2_kernel(x1_ref, o1_ref, o4_ref, o16_ref, l1_ref, l4_ref, l16_ref, z_ref,
                     wglu_ref, bglu_ref, wout_ref, gpost_ref, g3_ref, wg_ref, wu_ref, wo_ref, p3_ref,
                     y_ref, o_scr, l_scr, x2_scr):
    tm = TOKEN_TILE
    step = pl.program_id(0)

    @pl.when(step == 0)
    def _init():
        x2_scr[...] = jnp.zeros(x2_scr.shape, F32)

    prev_slot = lax.rem(step + 1, 2)
    h3 = _rms(x2_scr[prev_slot], g3_ref[...]).astype(BF16)
    y_ref[...] = _swiglu_ffn(lambda: x2_scr[prev_slot], h3, wg_ref, wu_ref, wo_ref, p3_ref[...])

    for n, (d, o_ref, l_ref) in enumerate(((4, o4_ref, l4_ref), (16, o16_ref, l16_ref))):
        for r in range(d):
            for c in range(ATTN_WIDTH // LANES):
                o_scr[n, c, pl.ds(r, tm // d, stride=d), :] = o_ref[0, r, :, c * LANES:(c + 1) * LANES].astype(F32)
            for c in range(STATS_WIDTH // LANES):
                l_scr[n, c, pl.ds(r, tm // d, stride=d), :] = l_ref[0, r, :, c * LANES:(c + 1) * LANES]
    ms = (l1_ref[...], l_scr[0, 0], l_scr[1, 0])
    ls = [pltpu.roll(st, LANES - L_LANE_OFFSET, axis=1) for st in ms]
    m = jnp.maximum(jnp.maximum(ms[0], ms[1]), ms[2])
    es = [jnp.exp2(mn - m) for mn in ms]
    inv = 1.0 / (es[0] * ls[0] + es[1] * ls[1] + es[2] * ls[2])
    ws = [e * inv for e in es]
    low_half = lax.broadcasted_iota(jnp.int32, (tm, 2 * HEAD_DIM), 1) < HEAD_DIM
    pairs = []
    for hp in range(N_HEADS // 2):
        cs = slice(2 * HEAD_DIM * hp, 2 * HEAD_DIM * (hp + 1))
        os_ = (o1_ref[:, cs].astype(F32), o_scr[0, hp], o_scr[1, hp])
        acc = jnp.zeros((tm, 2 * HEAD_DIM), F32)
        for w, o in zip(ws, os_):
            wexp = jnp.where(low_half, w[:, hp:hp + 1], w[:, HEAD_DIM + hp:HEAD_DIM + hp + 1])
            acc = acc + wexp * o
        pairs.append(acc)
    attn = jnp.concatenate(pairs, axis=-1).astype(BF16)
    z = z_ref[...]
    gate = jax.nn.sigmoid(jnp.dot(z, wglu_ref[...], preferred_element_type=F32) + bglu_ref[...])
    ssm = (z.astype(F32) * gate).astype(BF16)
    mixed = (jnp.dot(attn, wout_ref[0:ATTN_WIDTH, :], preferred_element_type=F32)
             + jnp.dot(ssm, wout_ref[ATTN_WIDTH:, :], preferred_element_type=F32))
    x2_scr[lax.rem(step, 2)] = x1_ref[...] + _rms(mixed, gpost_ref[...])


def _mix_ffn2(x1, o1, o4, o16, l1, l4, l16, z, wglu, bglu, wout, gpost, g3, wg, wu, wo, p3, seq_len):
    rows = x1.shape[0]
    tm = TOKEN_TILE
    n_tiles = rows // tm
    tok_now, res_now = _token_specs(n_tiles, seq_len // tm, 0)
    tok_prev, _ = _token_specs(n_tiles, seq_len // tm, 1)
    return pl.pallas_call(
        _mix_ffn2_kernel,
        grid=(n_tiles + 1,),
        in_specs=[tok_now(D_MODEL), tok_now(ATTN_WIDTH), res_now(4, ATTN_WIDTH), res_now(16, ATTN_WIDTH),
                  tok_now(STATS_WIDTH), res_now(4, STATS_WIDTH), res_now(16, STATS_WIDTH), tok_now(SSM_WIDTH),
                  _const_spec((SSM_WIDTH, SSM_WIDTH)), _const_spec((1, SSM_WIDTH)),
                  _const_spec((D_MODEL, D_MODEL)), _const_spec((1, D_MODEL)), _const_spec((1, D_MODEL)),
                  _const_spec((D_MODEL, D_FF)), _const_spec((D_MODEL, D_FF)), _const_spec((D_FF, D_MODEL)),
                  _const_spec((1, D_MODEL))],
        out_specs=tok_prev(D_MODEL),
        out_shape=jax.ShapeDtypeStruct((rows, D_MODEL), F32),
        scratch_shapes=[pltpu.VMEM((2, ATTN_WIDTH // LANES, tm, LANES), F32),
                        pltpu.VMEM((2, STATS_WIDTH // LANES, tm, LANES), F32),
                        pltpu.VMEM((2, tm, D_MODEL), F32)],
        compiler_params=pltpu.CompilerParams(
            dimension_semantics=("arbitrary",), vmem_limit_bytes=VMEM_LIMIT_BYTES),
        name="mix_ffn2",
    )(x1, o1, o4, o16, l1, l4, l16, z, wglu, bglu, wout, gpost, g3, wg, wu, wo, p3)


def _split_ffn_weights(w_in, w_out):
    return w_in[:, :D_FF].astype(BF16), w_in[:, D_FF:].astype(BF16), w_out.astype(BF16)


def _row(v):
    return v.astype(F32).reshape(1, -1)


def kernel(x, ffn1_pre_g, ffn1_w_in, ffn1_w_out, ffn1_post_g, mix_pre_g, w_mix_in, a_re, a_im, log_dt, b_re, b_im, c_re, c_im, d_skip, w_glu, b_glu, w_mix_out, mix_post_g, ffn2_pre_g, ffn2_w_in, ffn2_w_out, ffn2_post_g):
    B, S, _ = x.shape
    depth = ffn1_pre_g.shape[0]
    n_chunks = S // SSM_CHUNK
    for l in range(depth):
        wg1, wu1, wo1 = _split_ffn_weights(ffn1_w_in[l], ffn1_w_out[l])
        wg2, wu2, wo2 = _split_ffn_weights(ffn2_w_in[l], ffn2_w_out[l])
        (x1, q1, k1, v1, q4, k4, v4, q16, k16, v16, u) = _ffn1_proj(
            x, _row(ffn1_pre_g[l]), wg1, wu1, wo1, _row(ffn1_post_g[l]), _row(mix_pre_g[l]),
            w_mix_in[l].astype(BF16))

        outs = []
        for d, (q, k, v) in zip(DILATIONS, ((q1, k1, v1), (q4, k4, v4), (q16, k16, v16))):
            flat = lambda t: t.reshape(B * S, ATTN_WIDTH)
            outs.append(_attention_branch(flat(q), flat(k), flat(v), S // d, d))
        (o1, l1), (o4, l4), (o16, l16) = outs
        o4 = o4.reshape(B, 4, S // 4, ATTN_WIDTH)
        l4 = l4.reshape(B, 4, S // 4, STATS_WIDTH)
        o16 = o16.reshape(B, 16, S // 16, ATTN_WIDTH)
        l16 = l16.reshape(B, 16, S // 16, STATS_WIDTH)

        colp, rowp, dcol = _pack_ssm_params(a_re[l], a_im[l], log_dt[l], b_re[l], b_im[l], c_re[l], c_im[l],
                                            d_skip[l])
        ut = u.reshape(B, n_chunks, SSM_CHUNK, SSM_WIDTH).transpose(3, 2, 0, 1)
        ut = ut.reshape(SSM_WIDTH, SSM_CHUNK, B * n_chunks)
        zt = _ssm(ut, colp, rowp, dcol, n_chunks)
        z = zt.reshape(SSM_WIDTH, SSM_CHUNK, B, n_chunks).transpose(2, 3, 1, 0).reshape(B * S, SSM_WIDTH)

        x = _mix_ffn2(x1, o1, o4, o16, l1, l4, l16, z, w_glu[l].astype(BF16), _row(b_glu[l]),
                      w_mix_out[l].astype(BF16), _row(mix_post_g[l]), _row(ffn2_pre_g[l]), wg2, wu2, wo2,
                      _row(ffn2_post_g[l]), S).reshape(B, S, D_MODEL)
    return x
```

```python
import functools

import jax
import jax.numpy as jnp
from jax import lax
from jax.experimental import pallas as pl
from jax.experimental.pallas import tpu as pltpu

F32 = jnp.float32
BF16 = jnp.bfloat16

D_MODEL = 1024
ATTN_WIDTH = 512
SSM_WIDTH = 512
HEAD_DIM = 64
N_HEADS = 8
DILATIONS = (1, 4, 16)
WINDOW_STEPS = 128
QBLK = 128
SSM_GROUP = 16
N_SSM_GROUPS = 32
STATE_DIM = 64
D_FF = 2816
NORM_EPS = 1e-6

TOKEN_TILE = 512
FF_CHUNKS = (256,) * 11
SSM_CHUNK = 64
SSM_GROUPS_PER_STEP = 2
ATTN_TILE = 2048
MASK_VALUE = float("-inf")
LANES = 128
LOG2_E = 1.4426950408889634
QK_SCALE_LOG2 = HEAD_DIM ** -0.5 * LOG2_E
STATS_WIDTH = 2 * HEAD_DIM
L_LANE_OFFSET = 16
PROJ_SLABS = 4 * ATTN_WIDTH // LANES
VMEM_LIMIT_BYTES = 56 * 1024 * 1024


def _rms(x, g):
    return x * lax.rsqrt(jnp.mean(x * x, axis=-1, keepdims=True) + NORM_EPS) * g


def _swiglu_chunks(h, acc, chunks, wg_ref, wu_ref, wo_ref):
    for start, width in chunks:
        sl = slice(start, start + width)
        gate = jnp.dot(h, wg_ref[:, sl], preferred_element_type=F32)
        up = jnp.dot(h, wu_ref[:, sl], preferred_element_type=F32)
        act = (gate * jax.nn.sigmoid(gate) * up).astype(BF16)
        acc = acc + jnp.dot(act, wo_ref[sl, :], preferred_element_type=F32)
    return acc


def _ff_chunks():
    starts = [sum(FF_CHUNKS[:j]) for j in range(len(FF_CHUNKS))]
    return list(zip(starts, FF_CHUNKS))


def _swiglu_ffn(read_x, h, win_ref, wo_ref, post_g):
    wg_ref, wu_ref = win_ref.at[:, 0:D_FF], win_ref.at[:, D_FF:2 * D_FF]
    acc = _swiglu_chunks(h, jnp.zeros((h.shape[0], D_MODEL), F32), _ff_chunks(), wg_ref, wu_ref, wo_ref)
    return read_x() + 0.5 * _rms(acc, post_g)


def _const_spec(shape):
    return pl.BlockSpec(shape, lambda *_: (0,) * len(shape), pipeline_mode=pl.Buffered(1))


def _token_specs(n_tiles, tiles_per_seq, lag):
    tm = TOKEN_TILE

    def tile(i):
        return jnp.clip(i - lag, 0, n_tiles - 1)

    def tok(w):
        return pl.BlockSpec((tm, w), lambda i: (tile(i), 0))

    def res(d, w):
        return pl.BlockSpec((1, d, tm // d, w),
                            lambda i: (tile(i) // tiles_per_seq, 0, tile(i) % tiles_per_seq, 0))

    return tok, res


def _ffn1_proj_kernel(x_ref, g1_ref, win_ref, wo_ref, p1_ref, gm_ref, wm_ref,
                      x1_ref, q1_ref, k1_ref, v1_ref, q4_ref, k4_ref, v4_ref,
                      q16_ref, k16_ref, v16_ref, u_ref, proj_scr, mod4_scr):
    @pl.when(pl.program_id(0) == 0)
    def _init():
        proj_scr[...] = jnp.zeros(proj_scr.shape, F32)

    lanes_per = ATTN_WIDTH // LANES
    q4_rows = TOKEN_TILE // 4
    outs = ((q1_ref, q4_ref, q16_ref), (k1_ref, k4_ref, k16_ref), (v1_ref, v4_ref, v16_ref))
    for t, (o1, o4, o16) in enumerate(outs):
        for c in range(lanes_per):
            cs = slice(c * LANES, (c + 1) * LANES)
            slab = t * lanes_per + c
            o1[:, cs] = proj_scr[slab].astype(BF16)
            for b in range(4):
                cls4 = proj_scr[slab, pl.ds(b, q4_rows, stride=4), :]
                o4[0, b, :, cs] = cls4.astype(BF16)
                mod4_scr[slab, b * q4_rows:(b + 1) * q4_rows, :] = cls4
            for b in range(4):
                for a in range(4):
                    o16[0, 4 * a + b, :, cs] = mod4_scr[
                        slab, pl.ds(b * q4_rows + a, q4_rows // 4, stride=4), :].astype(BF16)
    for c in range(lanes_per):
        u_ref[:, c * LANES:(c + 1) * LANES] = proj_scr[3 * lanes_per + c].astype(BF16)

    h1 = _rms(x_ref[...], g1_ref[...]).astype(BF16)
    x1 = _swiglu_ffn(lambda: x_ref[...], h1, win_ref, wo_ref, p1_ref[...])
    x1_ref[...] = x1
    h = _rms(x1, gm_ref[...]).astype(BF16)
    proj = jnp.dot(h, wm_ref[...], preferred_element_type=F32)
    for cb in range(PROJ_SLABS):
        slab = proj[:, cb * LANES:(cb + 1) * LANES]
        proj_scr[cb] = slab * QK_SCALE_LOG2 if cb < lanes_per else slab


def _ffn1_proj(x, g1, win, wo, p1, gm, wm):
    B, S, _ = x.shape
    tm = TOKEN_TILE
    n_tiles = B * S // tm
    tok_now, _ = _token_specs(n_tiles, S // tm, 0)
    tok_prev, res_prev = _token_specs(n_tiles, S // tm, 1)
    nat = jax.ShapeDtypeStruct((B * S, ATTN_WIDTH), BF16)
    r4 = jax.ShapeDtypeStruct((B, 4, S // 4, ATTN_WIDTH), BF16)
    r16 = jax.ShapeDtypeStruct((B, 16, S // 16, ATTN_WIDTH), BF16)
    return pl.pallas_call(
        _ffn1_proj_kernel,
        grid=(n_tiles + 1,),
        in_specs=[tok_now(D_MODEL), _const_spec((1, D_MODEL)),
                  _const_spec((D_MODEL, 2 * D_FF)), _const_spec((D_FF, D_MODEL)),
                  _const_spec((1, D_MODEL)), _const_spec((1, D_MODEL)), _const_spec((D_MODEL, 4 * ATTN_WIDTH))],
        out_specs=[tok_now(D_MODEL)] + [tok_prev(ATTN_WIDTH)] * 3 + [res_prev(4, ATTN_WIDTH)] * 3
        + [res_prev(16, ATTN_WIDTH)] * 3 + [tok_prev(SSM_WIDTH)],
        out_shape=[jax.ShapeDtypeStruct((B * S, D_MODEL), F32)] + [nat] * 3 + [r4] * 3 + [r16] * 3 + [nat],
        scratch_shapes=[pltpu.VMEM((PROJ_SLABS, tm, LANES), F32),
                        pltpu.VMEM((3 * ATTN_WIDTH // LANES, tm, LANES), F32)],
        compiler_params=pltpu.CompilerParams(
            dimension_semantics=("arbitrary",), vmem_limit_bytes=VMEM_LIMIT_BYTES),
        name="ffn1_proj",
    )(x.reshape(B * S, D_MODEL), g1, win, wo, p1, gm, wm)


def _attn_kernel(q_ref, kc_ref, kp_ref, vc_ref, vp_ref, o_ref, st_ref, bias_ref, *, blocks_per_seq, dilation):
    @pl.when(pl.program_id(0) == 0)
    def _build_bias_tables():
        qi = lax.broadcasted_iota(jnp.int32, (QBLK, 2 * QBLK), 0)
        ci = lax.broadcasted_iota(jnp.int32, (QBLK, 2 * QBLK), 1)
        steps = QBLK + qi - ci
        dist = (steps * dilation).astype(F32)
        band = jnp.where(steps >= 0, jnp.where(steps <= WINDOW_STEPS, 1, 0), 0)
        band_first = jnp.where(ci >= QBLK, band, 0)
        for h in range(N_HEADS):
            bias = -(2.0 ** (-8.0 * (h + 1) / N_HEADS) * LOG2_E) * dist
            bias_ref[0, h] = jnp.where(band == 1, bias, MASK_VALUE)
            bias_ref[1, h] = jnp.where(band_first == 1, bias, MASK_VALUE)

    n_blocks = ATTN_TILE // QBLK
    lane = lax.broadcasted_iota(jnp.int32, (1, 2 * HEAD_DIM), 1)
    head_mask = ((lane < HEAD_DIM).astype(BF16), (lane >= HEAD_DIM).astype(BF16))
    ones_cols = tuple(jnp.broadcast_to(mk, (2 * QBLK, 2 * HEAD_DIM)) for mk in head_mask)
    lane_f = lax.broadcasted_iota(jnp.int32, (QBLK, 2 * HEAD_DIM), 1)
    low_half = lane_f < HEAD_DIM

    for j in range(n_blocks):
        rows = slice(j * QBLK, (j + 1) * QBLK)
        if blocks_per_seq >= n_blocks:
            tiles_per_seq = blocks_per_seq // n_blocks
            table = (lax.rem(pl.program_id(0), tiles_per_seq) == 0).astype(jnp.int32) if j == 0 else 0
        else:
            table = 1 if j % blocks_per_seq == 0 else 0
        st_tile = jnp.ones((QBLK, 2 * HEAD_DIM), F32)
        for hp in range(N_HEADS // 2):
            cs = slice(2 * HEAD_DIM * hp, 2 * HEAD_DIM * (hp + 1))
            q = q_ref[rows, cs]
            if j == 0:
                kk = jnp.concatenate([kp_ref[:, cs], kc_ref[0:QBLK, cs]], axis=0)
                vv = jnp.concatenate([vp_ref[:, cs], vc_ref[0:QBLK, cs]], axis=0)
            else:
                kk = kc_ref[(j - 1) * QBLK:(j + 1) * QBLK, cs]
                vv = vc_ref[(j - 1) * QBLK:(j + 1) * QBLK, cs]
            ps, ms, vas = [], [], []
            for e in range(2):
                s = lax.dot_general(q * head_mask[e], kk, (((1,), (1,)), ((), ())),
                                    preferred_element_type=F32)
                s = s + bias_ref[table, 2 * hp + e]
                m = jnp.max(s, axis=-1, keepdims=True)
                ps.append(jnp.exp2(s - m).astype(BF16))
                ms.append(m)
                vas.append(jnp.concatenate([vv * head_mask[e], ones_cols[e]], axis=1))
            pv = jnp.dot(jnp.concatenate(ps, axis=1), jnp.concatenate(vas, axis=0),
                         preferred_element_type=F32)
            o_ref[rows, cs] = pv[:, :2 * HEAD_DIM].astype(BF16)
            in_half = lane_f & (HEAD_DIM - 1)
            st_tile = jnp.where(in_half == hp, jnp.where(low_half, ms[0], ms[1]),
                                jnp.where(in_half == hp + L_LANE_OFFSET, pv[:, 2 * HEAD_DIM:], st_tile))
        st_ref[rows, :] = st_tile


def _attention_branch(q, k, v, seq_len, dilation):
    rows = q.shape[0]
    ratio = ATTN_TILE // QBLK
    cur = pl.BlockSpec((ATTN_TILE, ATTN_WIDTH), lambda i: (i, 0))
    prev = pl.BlockSpec((QBLK, ATTN_WIDTH), lambda i: (jnp.maximum(i * ratio - 1, 0), 0))
    return pl.pallas_call(
        functools.partial(_attn_kernel, blocks_per_seq=seq_len // QBLK, dilation=dilation),
        grid=(rows // ATTN_TILE,),
        in_specs=[cur, cur, prev, cur, prev],
        out_specs=[cur, pl.BlockSpec((ATTN_TILE, STATS_WIDTH), lambda i: (i, 0))],
        out_shape=[jax.ShapeDtypeStruct((rows, ATTN_WIDTH), BF16),
                   jax.ShapeDtypeStruct((rows, STATS_WIDTH), F32)],
        scratch_shapes=[pltpu.VMEM((2, N_HEADS, QBLK, 2 * QBLK), F32)],
        compiler_params=pltpu.CompilerParams(dimension_semantics=("arbitrary",)),
        name=f"attn_s{seq_len}",
    )(q, k, k, v, v)


def _cexp(n, lam_re, lam_im):
    mag = jnp.exp(n * lam_re)
    return mag * jnp.cos(n * lam_im), mag * jnp.sin(n * lam_im)


def _ssm_kernel(u_ref, colp_ref, rowp_ref, dcol_ref, y_ref, a1_scr, cw_scr, *, chunks_per_seq):
    for gi in range(SSM_GROUPS_PER_STEP):
        rows = pl.ds(gi * SSM_GROUP, SSM_GROUP)
        _ssm_group(u_ref.at[:, rows], colp_ref.at[gi], rowp_ref.at[gi], dcol_ref.at[gi], y_ref.at[rows],
                   a1_scr.at[gi], cw_scr.at[gi], chunks_per_seq)


def _ssm_group(u_ref, colp_ref, rowp_ref, dcol_ref, y_ref, a1_scr, cw_scr, chunks_per_seq):
    T, P, C = SSM_CHUNK, STATE_DIM, SSM_GROUP
    n_rows = C * T
    n_cols = y_ref.shape[-1]
    lane = lax.broadcasted_iota(jnp.int32, (1, LANES), 1)
    low = lane < P

    a_re = colp_ref[:, 0:1]
    a_im = colp_ref[:, 1:2]
    dt = jnp.exp(colp_ref[:, 34:35])
    lam_re, lam_im = dt * a_re, dt * a_im
    ab_re, ab_im = _cexp(1.0, lam_re, lam_im)
    inv_a2 = 1.0 / (a_re * a_re + a_im * a_im)
    nr, ni = ab_re - 1.0, ab_im
    cf_re = (nr * a_re + ni * a_im) * inv_a2
    cf_im = (ni * a_re - nr * a_im) * inv_a2

    rev = (T - 1 - (lane & (T - 1))).astype(F32)
    pw_re, pw_im = _cexp(rev, lam_re, lam_im)
    g_re, g_im = [], []
    for k in range(C // 2):
        b_re = jnp.where(low, colp_ref[:, 2 + 2 * k:3 + 2 * k], colp_ref[:, 3 + 2 * k:4 + 2 * k])
        b_im = jnp.where(low, colp_ref[:, 18 + 2 * k:19 + 2 * k], colp_ref[:, 19 + 2 * k:20 + 2 * k])
        bb_re = cf_re * b_re - cf_im * b_im
        bb_im = cf_re * b_im + cf_im * b_re
        g_re.append(pw_re * bb_re - pw_im * bb_im)
        g_im.append(pw_re * bb_im + pw_im * bb_re)
    gm = jnp.concatenate([jnp.concatenate(g_re, axis=1), jnp.concatenate(g_im, axis=1)], axis=0)
    a1_scr[n_rows:, :] = gm.astype(BF16)

    c_re2 = rowp_ref[0:C, :]
    c_im2 = rowp_ref[C:2 * C, :]
    kr = jnp.dot(jnp.where(low, c_re2, -c_im2), gm, precision=lax.Precision.HIGHEST,
                 preferred_element_type=F32)
    krow = lax.broadcasted_iota(jnp.int32, (C, C * T), 0)
    klane = lax.broadcasted_iota(jnp.int32, (C, C * T), 1)
    kr = kr + jnp.where(klane == krow * T + (T - 1), dcol_ref[:, 0:1], 0.0)

    t_idx = lax.broadcasted_iota(jnp.int32, (T, LANES), 0)
    s_idx = lax.broadcasted_iota(jnp.int32, (T, LANES), 1) & (T - 1)
    causal = s_idx <= t_idx
    for c in range(C):
        for k in range(C * T // LANES):
            z = jnp.broadcast_to(kr[c:c + 1, k * LANES:(k + 1) * LANES], (T, LANES))
            z = pltpu.roll(z, LANES - (T - 1), axis=1, stride=1, stride_axis=0)
            a1_scr[c * T:(c + 1) * T, k * LANES:(k + 1) * LANES] = jnp.where(causal, z, 0.0).astype(BF16)

    dt_r = jnp.exp(rowp_ref[34:35, :])
    steps = (lax.broadcasted_iota(jnp.int32, (T, LANES), 0) + 1).astype(F32)
    pr, pi = _cexp(steps, dt_r * rowp_ref[32:33, :], dt_r * rowp_ref[33:34, :])
    for c in range(C):
        cr, ci = c_re2[c:c + 1, :], c_im2[c:c + 1, :]
        cw_scr[c * T:(c + 1) * T, :] = jnp.where(low, cr * pr - ci * pi, -(cr * pi + ci * pr)).astype(BF16)

    u = jnp.concatenate([u_ref[b].reshape(n_rows, chunks_per_seq) for b in range(u_ref.shape[0])], axis=1)
    y1 = jnp.dot(a1_scr[...], u, preferred_element_type=F32)
    xr = y1[n_rows:n_rows + P]
    xi = y1[n_rows + P:]
    kidx = lax.broadcasted_iota(jnp.int32, (P, n_cols), 1) & (chunks_per_seq - 1)
    mr, mi = _cexp(float(T), lam_re, lam_im)
    sh = 1
    while sh < chunks_per_seq:
        rr = pltpu.roll(xr, sh, axis=1)
        ri = pltpu.roll(xi, sh, axis=1)
        ok = kidx >= sh
        xr, xi = (xr + jnp.where(ok, mr * rr - mi * ri, 0.0),
                  xi + jnp.where(ok, mr * ri + mi * rr, 0.0))
        mr, mi = mr * mr - mi * mi, 2.0 * mr * mi
        sh *= 2
    ok = kidx >= 1
    hr = jnp.where(ok, pltpu.roll(xr, 1, axis=1), 0.0)
    hi = jnp.where(ok, pltpu.roll(xi, 1, axis=1), 0.0)
    h = jnp.concatenate([hr, hi], axis=0).astype(BF16)
    y = y1[:n_rows] + jnp.dot(cw_scr[...], h, preferred_element_type=F32)
    y_ref[...] = jax.nn.gelu(y).astype(BF16).reshape(C, T, n_cols)


def _ssm(ut, colp, rowp, dcol):
    batch, _, _, chunks_per_seq = ut.shape
    n_cols = batch * chunks_per_seq
    n_rows = SSM_GROUP * SSM_CHUNK
    per_step = SSM_GROUPS_PER_STEP
    u_blk = pl.BlockSpec((batch, per_step * SSM_GROUP, SSM_CHUNK, chunks_per_seq), lambda g: (0, g, 0, 0))
    y_blk = pl.BlockSpec((per_step * SSM_GROUP, SSM_CHUNK, n_cols), lambda g: (g, 0, 0))
    per_group = lambda a: pl.BlockSpec((per_step,) + a.shape[1:], lambda g: (g, 0, 0))
    return pl.pallas_call(
        functools.partial(_ssm_kernel, chunks_per_seq=chunks_per_seq),
        grid=(N_SSM_GROUPS // per_step,),
        in_specs=[u_blk, per_group(colp), per_group(rowp), per_group(dcol)],
        out_specs=y_blk,
        out_shape=jax.ShapeDtypeStruct((SSM_WIDTH, SSM_CHUNK, n_cols), BF16),
        scratch_shapes=[pltpu.VMEM((per_step, n_rows + 2 * STATE_DIM, n_rows), BF16),
                        pltpu.VMEM((per_step, n_rows, 2 * STATE_DIM), BF16)],
        compiler_params=pltpu.CompilerParams(dimension_semantics=("arbitrary",)),
        name="ssm",
    )(ut, colp, rowp, dcol)


def _pack_ssm_params(a_re, a_im, log_dt, b_re, b_im, c_re, c_im, d_skip):
    G, P, C = N_SSM_GROUPS, STATE_DIM, SSM_GROUP
    f = lambda t: t.astype(F32)
    log_dt_col = jnp.broadcast_to(f(log_dt)[:, None, None], (G, P, 1))
    colp = jnp.concatenate([f(a_re)[:, :, None], f(a_im)[:, :, None], f(b_re), f(b_im), log_dt_col,
                            jnp.zeros((G, P, LANES - 3 - 2 * C), F32)], axis=2)
    twice = lambda t: jnp.concatenate([f(t), f(t)], axis=-1)
    rowp = jnp.concatenate([twice(c_re), twice(c_im), twice(a_re)[:, None, :], twice(a_im)[:, None, :],
                            jnp.broadcast_to(f(log_dt)[:, None, None], (G, 1, 2 * P)),
                            jnp.zeros((G, 5, 2 * P), F32)], axis=1)
    dcol = jnp.broadcast_to(f(d_skip).reshape(G, C, 1), (G, C, LANES))
    return colp, rowp, dcol


def _mix_ffn2_kernel(x1_ref, o1_ref, o4_ref, o16_ref, l1_ref, l4_ref, l16_ref, z_ref,
                     wglu_ref, bglu_ref, wout_ref, gpost_ref, g3_ref, win_ref, wo_ref, p3_ref,
                     y_ref, o_scr, l_scr, x2_scr):
    tm = TOKEN_TILE
    step = pl.program_id(0)

    @pl.when(step == 0)
    def _init():
        x2_scr[...] = jnp.zeros(x2_scr.shape, F32)

    prev_slot = lax.rem(step + 1, 2)
    h3 = _rms(x2_scr[prev_slot], g3_ref[...]).astype(BF16)
    y_ref[...] = _swiglu_ffn(lambda: x2_scr[prev_slot], h3, win_ref, wo_ref, p3_ref[...])

    for n, (d, o_ref, l_ref) in enumerate(((4, o4_ref, l4_ref), (16, o16_ref, l16_ref))):
        for r in range(d):
            for c in range(ATTN_WIDTH // LANES):
                o_scr[n, c, pl.ds(r, tm // d, stride=d), :] = o_ref[0, r, :, c * LANES:(c + 1) * LANES].astype(F32)
            for c in range(STATS_WIDTH // LANES):
                l_scr[n, c, pl.ds(r, tm // d, stride=d), :] = l_ref[0, r, :, c * LANES:(c + 1) * LANES]
    ms = (l1_ref[...], l_scr[0, 0], l_scr[1, 0])
    ls = [pltpu.roll(st, LANES - L_LANE_OFFSET, axis=1) for st in ms]
    m = jnp.maximum(jnp.maximum(ms[0], ms[1]), ms[2])
    es = [jnp.exp2(mn - m) for mn in ms]
    inv = 1.0 / (es[0] * ls[0] + es[1] * ls[1] + es[2] * ls[2])
    ws = [e * inv for e in es]
    low_half = lax.broadcasted_iota(jnp.int32, (tm, 2 * HEAD_DIM), 1) < HEAD_DIM
    pairs = []
    for hp in range(N_HEADS // 2):
        cs = slice(2 * HEAD_DIM * hp, 2 * HEAD_DIM * (hp + 1))
        os_ = (o1_ref[:, cs].astype(F32), o_scr[0, hp], o_scr[1, hp])
        acc = jnp.zeros((tm, 2 * HEAD_DIM), F32)
        for w, o in zip(ws, os_):
            wexp = jnp.where(low_half, w[:, hp:hp + 1], w[:, HEAD_DIM + hp:HEAD_DIM + hp + 1])
            acc = acc + wexp * o
        pairs.append(acc)
    attn = jnp.concatenate(pairs, axis=-1).astype(BF16)
    z = z_ref[...]
    gate = jax.nn.sigmoid(jnp.dot(z, wglu_ref[...], preferred_element_type=F32) + bglu_ref[...])
    ssm = (z.astype(F32) * gate).astype(BF16)
    mixed = (jnp.dot(attn, wout_ref[0:ATTN_WIDTH, :], preferred_element_type=F32)
             + jnp.dot(ssm, wout_ref[ATTN_WIDTH:, :], preferred_element_type=F32))
    x2_scr[lax.rem(step, 2)] = x1_ref[...] + _rms(mixed, gpost_ref[...])


def _mix_ffn2(x1, o1, o4, o16, l1, l4, l16, z, wglu, bglu, wout, gpost, g3, win, wo, p3, seq_len):
    rows = x1.shape[0]
    tm = TOKEN_TILE
    n_tiles = rows // tm
    tok_now, res_now = _token_specs(n_tiles, seq_len // tm, 0)
    tok_prev, _ = _token_specs(n_tiles, seq_len // tm, 1)
    return pl.pallas_call(
        _mix_ffn2_kernel,
        grid=(n_tiles + 1,),
        in_specs=[tok_now(D_MODEL), tok_now(ATTN_WIDTH), res_now(4, ATTN_WIDTH), res_now(16, ATTN_WIDTH),
                  tok_now(STATS_WIDTH), res_now(4, STATS_WIDTH), res_now(16, STATS_WIDTH), tok_now(SSM_WIDTH),
                  _const_spec((SSM_WIDTH, SSM_WIDTH)), _const_spec((1, SSM_WIDTH)),
                  _const_spec((D_MODEL, D_MODEL)), _const_spec((1, D_MODEL)), _const_spec((1, D_MODEL)),
                  _const_spec((D_MODEL, 2 * D_FF)), _const_spec((D_FF, D_MODEL)),
                  _const_spec((1, D_MODEL))],
        out_specs=tok_prev(D_MODEL),
        out_shape=jax.ShapeDtypeStruct((rows, D_MODEL), F32),
        scratch_shapes=[pltpu.VMEM((2, ATTN_WIDTH // LANES, tm, LANES), F32),
                        pltpu.VMEM((2, STATS_WIDTH // LANES, tm, LANES), F32),
                        pltpu.VMEM((2, tm, D_MODEL), F32)],
        compiler_params=pltpu.CompilerParams(
            dimension_semantics=("arbitrary",), vmem_limit_bytes=VMEM_LIMIT_BYTES),
        name="mix_ffn2",
    )(x1, o1, o4, o16, l1, l4, l16, z, wglu, bglu, wout, gpost, g3, win, wo, p3)


def _row(v):
    return v.astype(F32).reshape(1, -1)


def kernel(x, ffn1_pre_g, ffn1_w_in, ffn1_w_out, ffn1_post_g, mix_pre_g, w_mix_in, a_re, a_im, log_dt, b_re, b_im, c_re, c_im, d_skip, w_glu, b_glu, w_mix_out, mix_post_g, ffn2_pre_g, ffn2_w_in, ffn2_w_out, ffn2_post_g):
    B, S, _ = x.shape
    depth = ffn1_pre_g.shape[0]
    n_chunks = S // SSM_CHUNK
    for l in range(depth):
        (x1, q1, k1, v1, q4, k4, v4, q16, k16, v16, u) = _ffn1_proj(
            x, _row(ffn1_pre_g[l]), ffn1_w_in[l].astype(BF16), ffn1_w_out[l].astype(BF16),
            _row(ffn1_post_g[l]), _row(mix_pre_g[l]), w_mix_in[l].astype(BF16))

        outs = []
        for d, (q, k, v) in zip(DILATIONS, ((q1, k1, v1), (q4, k4, v4), (q16, k16, v16))):
            flat = lambda t: t.reshape(B * S, ATTN_WIDTH)
            outs.append(_attention_branch(flat(q), flat(k), flat(v), S // d, d))
        (o1, l1), (o4, l4), (o16, l16) = outs
        o4 = o4.reshape(B, 4, S // 4, ATTN_WIDTH)
        l4 = l4.reshape(B, 4, S // 4, STATS_WIDTH)
        o16 = o16.reshape(B, 16, S // 16, ATTN_WIDTH)
        l16 = l16.reshape(B, 16, S // 16, STATS_WIDTH)

        colp, rowp, dcol = _pack_ssm_params(a_re[l], a_im[l], log_dt[l], b_re[l], b_im[l], c_re[l], c_im[l],
                                            d_skip[l])
        ut = u.reshape(B, n_chunks, SSM_CHUNK, SSM_WIDTH).transpose(0, 3, 2, 1)
        zt = _ssm(ut, colp, rowp, dcol)
        z = zt.reshape(SSM_WIDTH, SSM_CHUNK, B, n_chunks).transpose(2, 3, 1, 0).reshape(B * S, SSM_WIDTH)

        x = _mix_ffn2(x1, o1, o4, o16, l1, l4, l16, z, w_glu[l].astype(BF16), _row(b_glu[l]),
                      w_mix_out[l].astype(BF16), _row(mix_post_g[l]), _row(ffn2_pre_g[l]),
                      ffn2_w_in[l].astype(BF16), ffn2_w_out[l].astype(BF16), _row(ffn2_post_g[l]),
                      S).reshape(B, S, D_MODEL)
    return x
```

```python
import functools

import jax
import jax.numpy as jnp
from jax import lax
from jax.experimental import pallas as pl
from jax.experimental.pallas import tpu as pltpu

F32 = jnp.float32
BF16 = jnp.bfloat16

D_MODEL = 1024
ATTN_WIDTH = 512
SSM_WIDTH = 512
HEAD_DIM = 64
N_HEADS = 8
DILATIONS = (1, 4, 16)
WINDOW_STEPS = 128
QBLK = 128
SSM_GROUP = 16
N_SSM_GROUPS = 32
STATE_DIM = 64
D_FF = 2816
NORM_EPS = 1e-6

TOKEN_TILE = 512
FF_CHUNKS = (256,) * 11
SSM_CHUNK = 64
SSM_GROUPS_PER_STEP = 2
ATTN_TILE = 2048
MASK_VALUE = float("-inf")
LANES = 128
LOG2_E = 1.4426950408889634
QK_SCALE_LOG2 = HEAD_DIM ** -0.5 * LOG2_E
STATS_WIDTH = 2 * HEAD_DIM
L_LANE_OFFSET = 16
PROJ_SLABS = 4 * ATTN_WIDTH // LANES
VMEM_LIMIT_BYTES = 56 * 1024 * 1024


def _rms(x, g):
    return x * lax.rsqrt(jnp.mean(x * x, axis=-1, keepdims=True) + NORM_EPS) * g


def _swiglu_chunks(h, acc, chunks, wg_ref, wu_ref, wo_ref):
    for start, width in chunks:
        sl = slice(start, start + width)
        gate = jnp.dot(h, wg_ref[:, sl], preferred_element_type=F32)
        up = jnp.dot(h, wu_ref[:, sl], preferred_element_type=F32)
        act = (gate * jax.nn.sigmoid(gate) * up).astype(BF16)
        acc = acc + jnp.dot(act, wo_ref[sl, :], preferred_element_type=F32)
    return acc


def _ff_chunks():
    starts = [sum(FF_CHUNKS[:j]) for j in range(len(FF_CHUNKS))]
    return list(zip(starts, FF_CHUNKS))


def _swiglu_ffn(read_x, h, win_ref, wo_ref, post_g):
    wg_ref, wu_ref = win_ref.at[:, 0:D_FF], win_ref.at[:, D_FF:2 * D_FF]
    acc = _swiglu_chunks(h, jnp.zeros((h.shape[0], D_MODEL), F32), _ff_chunks(), wg_ref, wu_ref, wo_ref)
    return read_x() + 0.5 * _rms(acc, post_g)


def _const_spec(shape):
    return pl.BlockSpec(shape, lambda *_: (0,) * len(shape), pipeline_mode=pl.Buffered(1))


def _token_specs(n_tiles, tiles_per_seq, lag):
    tm = TOKEN_TILE

    def tile(i):
        return jnp.clip(i - lag, 0, n_tiles - 1)

    def tok(w):
        return pl.BlockSpec((tm, w), lambda i: (tile(i), 0))

    def res(d, w):
        return pl.BlockSpec((1, d, tm // d, w),
                            lambda i: (tile(i) // tiles_per_seq, 0, tile(i) % tiles_per_seq, 0))

    return tok, res


def _ffn1_proj_kernel(x_ref, g1_ref, win_ref, wo_ref, p1_ref, gm_ref, wm_ref,
                      x1_ref, q1_ref, k1_ref, v1_ref, q4_ref, k4_ref, v4_ref,
                      q16_ref, k16_ref, v16_ref, u_ref, proj_scr, mod4_scr):
    @pl.when(pl.program_id(0) == 0)
    def _init():
        proj_scr[...] = jnp.zeros(proj_scr.shape, F32)

    lanes_per = ATTN_WIDTH // LANES
    q4_rows = TOKEN_TILE // 4
    outs = ((q1_ref, q4_ref, q16_ref), (k1_ref, k4_ref, k16_ref), (v1_ref, v4_ref, v16_ref))
    for t, (o1, o4, o16) in enumerate(outs):
        for c in range(lanes_per):
            cs = slice(c * LANES, (c + 1) * LANES)
            slab = t * lanes_per + c
            o1[:, cs] = proj_scr[slab].astype(BF16)
            for b in range(4):
                cls4 = proj_scr[slab, pl.ds(b, q4_rows, stride=4), :]
                o4[0, b, :, cs] = cls4.astype(BF16)
                mod4_scr[slab, b * q4_rows:(b + 1) * q4_rows, :] = cls4
            for b in range(4):
                for a in range(4):
                    o16[0, 4 * a + b, :, cs] = mod4_scr[
                        slab, pl.ds(b * q4_rows + a, q4_rows // 4, stride=4), :].astype(BF16)
    for c in range(lanes_per):
        u_ref[:, c * LANES:(c + 1) * LANES] = proj_scr[3 * lanes_per + c].astype(BF16)

    h1 = _rms(x_ref[...], g1_ref[...]).astype(BF16)
    x1 = _swiglu_ffn(lambda: x_ref[...], h1, win_ref, wo_ref, p1_ref[...])
    x1_ref[...] = x1
    h = _rms(x1, gm_ref[...]).astype(BF16)
    proj = jnp.dot(h, wm_ref[...], preferred_element_type=F32)
    for cb in range(PROJ_SLABS):
        slab = proj[:, cb * LANES:(cb + 1) * LANES]
        proj_scr[cb] = slab * QK_SCALE_LOG2 if cb < lanes_per else slab


def _ffn1_proj(x, g1, win, wo, p1, gm, wm):
    B, S, _ = x.shape
    tm = TOKEN_TILE
    n_tiles = B * S // tm
    tok_now, _ = _token_specs(n_tiles, S // tm, 0)
    tok_prev, res_prev = _token_specs(n_tiles, S // tm, 1)
    nat = jax.ShapeDtypeStruct((B * S, ATTN_WIDTH), BF16)
    r4 = jax.ShapeDtypeStruct((B, 4, S // 4, ATTN_WIDTH), BF16)
    r16 = jax.ShapeDtypeStruct((B, 16, S // 16, ATTN_WIDTH), BF16)
    return pl.pallas_call(
        _ffn1_proj_kernel,
        grid=(n_tiles + 1,),
        in_specs=[tok_now(D_MODEL), _const_spec((1, D_MODEL)),
                  _const_spec((D_MODEL, 2 * D_FF)), _const_spec((D_FF, D_MODEL)),
                  _const_spec((1, D_MODEL)), _const_spec((1, D_MODEL)), _const_spec((D_MODEL, 4 * ATTN_WIDTH))],
        out_specs=[tok_now(D_MODEL)] + [tok_prev(ATTN_WIDTH)] * 3 + [res_prev(4, ATTN_WIDTH)] * 3
        + [res_prev(16, ATTN_WIDTH)] * 3 + [tok_prev(SSM_WIDTH)],
        out_shape=[jax.ShapeDtypeStruct((B * S, D_MODEL), F32)] + [nat] * 3 + [r4] * 3 + [r16] * 3 + [nat],
        scratch_shapes=[pltpu.VMEM((PROJ_SLABS, tm, LANES), F32),
                        pltpu.VMEM((3 * ATTN_WIDTH // LANES, tm, LANES), F32)],
        compiler_params=pltpu.CompilerParams(
            dimension_semantics=("arbitrary",), vmem_limit_bytes=VMEM_LIMIT_BYTES),
        name="ffn1_proj",
    )(x.reshape(B * S, D_MODEL), g1, win, wo, p1, gm, wm)


def _attn_kernel(q_ref, kc_ref, kp_ref, vc_ref, vp_ref, o_ref, st_ref, bias_ref, *, blocks_per_seq, dilation):
    @pl.when(pl.program_id(0) == 0)
    def _build_bias_tables():
        qi = lax.broadcasted_iota(jnp.int32, (QBLK, 2 * QBLK), 0)
        ci = lax.broadcasted_iota(jnp.int32, (QBLK, 2 * QBLK), 1)
        steps = QBLK + qi - ci
        dist = (steps * dilation).astype(F32)
        band = jnp.where(steps >= 0, jnp.where(steps <= WINDOW_STEPS, 1, 0), 0)
        band_first = jnp.where(ci >= QBLK, band, 0)
        for h in range(N_HEADS):
            bias = -(2.0 ** (-8.0 * (h + 1) / N_HEADS) * LOG2_E) * dist
            bias_ref[0, h] = jnp.where(band == 1, bias, MASK_VALUE)
            bias_ref[1, h] = jnp.where(band_first == 1, bias, MASK_VALUE)

    n_blocks = ATTN_TILE // QBLK
    lane = lax.broadcasted_iota(jnp.int32, (1, 2 * HEAD_DIM), 1)
    head_mask = ((lane < HEAD_DIM).astype(BF16), (lane >= HEAD_DIM).astype(BF16))
    ones_cols = tuple(jnp.broadcast_to(mk, (2 * QBLK, 2 * HEAD_DIM)) for mk in head_mask)
    lane_f = lax.broadcasted_iota(jnp.int32, (QBLK, 2 * HEAD_DIM), 1)
    low_half = lane_f < HEAD_DIM

    for j in range(n_blocks):
        rows = slice(j * QBLK, (j + 1) * QBLK)
        if blocks_per_seq >= n_blocks:
            tiles_per_seq = blocks_per_seq // n_blocks
            table = (lax.rem(pl.program_id(0), tiles_per_seq) == 0).astype(jnp.int32) if j == 0 else 0
        else:
            table = 1 if j % blocks_per_seq == 0 else 0
        st_tile = jnp.ones((QBLK, 2 * HEAD_DIM), F32)
        for hp in range(N_HEADS // 2):
            cs = slice(2 * HEAD_DIM * hp, 2 * HEAD_DIM * (hp + 1))
            q = q_ref[rows, cs]
            if j == 0:
                kk = jnp.concatenate([kp_ref[:, cs], kc_ref[0:QBLK, cs]], axis=0)
                vv = jnp.concatenate([vp_ref[:, cs], vc_ref[0:QBLK, cs]], axis=0)
            else:
                kk = kc_ref[(j - 1) * QBLK:(j + 1) * QBLK, cs]
                vv = vc_ref[(j - 1) * QBLK:(j + 1) * QBLK, cs]
            ps, ms, vas = [], [], []
            for e in range(2):
                s = lax.dot_general(q * head_mask[e], kk, (((1,), (1,)), ((), ())),
                                    preferred_element_type=F32)
                s = s + bias_ref[table, 2 * hp + e]
                m = jnp.max(s, axis=-1, keepdims=True)
                ps.append(jnp.exp2(s - m).astype(BF16))
                ms.append(m)
                vas.append(jnp.concatenate([vv * head_mask[e], ones_cols[e]], axis=1))
            pv = jnp.dot(jnp.concatenate(ps, axis=1), jnp.concatenate(vas, axis=0),
                         preferred_element_type=F32)
            o_ref[rows, cs] = pv[:, :2 * HEAD_DIM].astype(BF16)
            in_half = lane_f & (HEAD_DIM - 1)
            st_tile = jnp.where(in_half == hp, jnp.where(low_half, ms[0], ms[1]),
                                jnp.where(in_half == hp + L_LANE_OFFSET, pv[:, 2 * HEAD_DIM:], st_tile))
        st_ref[rows, :] = st_tile


def _attention_branch(q, k, v, seq_len, dilation):
    rows = q.shape[0]
    ratio = ATTN_TILE // QBLK
    cur = pl.BlockSpec((ATTN_TILE, ATTN_WIDTH), lambda i: (i, 0))
    prev = pl.BlockSpec((QBLK, ATTN_WIDTH), lambda i: (jnp.maximum(i * ratio - 1, 0), 0))
    return pl.pallas_call(
        functools.partial(_attn_kernel, blocks_per_seq=seq_len // QBLK, dilation=dilation),
        grid=(rows // ATTN_TILE,),
        in_specs=[cur, cur, prev, cur, prev],
        out_specs=[cur, pl.BlockSpec((ATTN_TILE, STATS_WIDTH), lambda i: (i, 0))],
        out_shape=[jax.ShapeDtypeStruct((rows, ATTN_WIDTH), BF16),
                   jax.ShapeDtypeStruct((rows, STATS_WIDTH), F32)],
        scratch_shapes=[pltpu.VMEM((2, N_HEADS, QBLK, 2 * QBLK), F32)],
        compiler_params=pltpu.CompilerParams(dimension_semantics=("arbitrary",)),
        name=f"attn_s{seq_len}",
    )(q, k, k, v, v)


def _cexp(n, lam_re, lam_im):
    mag = jnp.exp(n * lam_re)
    return mag * jnp.cos(n * lam_im), mag * jnp.sin(n * lam_im)


def _ssm_kernel(u_ref, colp_ref, rowp_ref, dcol_ref, y_ref, a1_scr, cw_scr, *, chunks_per_seq):
    for gi in range(SSM_GROUPS_PER_STEP):
        rows = pl.ds(gi * SSM_GROUP, SSM_GROUP)
        _ssm_group(u_ref.at[:, rows], colp_ref.at[gi], rowp_ref.at[gi], dcol_ref.at[gi], y_ref.at[:, rows],
                   a1_scr.at[gi], cw_scr.at[gi], chunks_per_seq)


def _ssm_group(u_ref, colp_ref, rowp_ref, dcol_ref, y_ref, a1_scr, cw_scr, chunks_per_seq):
    T, P, C = SSM_CHUNK, STATE_DIM, SSM_GROUP
    n_rows = C * T
    n_batch = u_ref.shape[0]
    n_cols = n_batch * chunks_per_seq
    lane = lax.broadcasted_iota(jnp.int32, (1, LANES), 1)
    low = lane < P

    a_re = colp_ref[:, 0:1]
    a_im = colp_ref[:, 1:2]
    dt = jnp.exp(colp_ref[:, 34:35])
    lam_re, lam_im = dt * a_re, dt * a_im
    ab_re, ab_im = _cexp(1.0, lam_re, lam_im)
    inv_a2 = 1.0 / (a_re * a_re + a_im * a_im)
    nr, ni = ab_re - 1.0, ab_im
    cf_re = (nr * a_re + ni * a_im) * inv_a2
    cf_im = (ni * a_re - nr * a_im) * inv_a2

    rev = (T - 1 - (lane & (T - 1))).astype(F32)
    pw_re, pw_im = _cexp(rev, lam_re, lam_im)
    g_re, g_im = [], []
    for k in range(C // 2):
        b_re = jnp.where(low, colp_ref[:, 2 + 2 * k:3 + 2 * k], colp_ref[:, 3 + 2 * k:4 + 2 * k])
        b_im = jnp.where(low, colp_ref[:, 18 + 2 * k:19 + 2 * k], colp_ref[:, 19 + 2 * k:20 + 2 * k])
        bb_re = cf_re * b_re - cf_im * b_im
        bb_im = cf_re * b_im + cf_im * b_re
        g_re.append(pw_re * bb_re - pw_im * bb_im)
        g_im.append(pw_re * bb_im + pw_im * bb_re)
    gm = jnp.concatenate([jnp.concatenate(g_re, axis=1), jnp.concatenate(g_im, axis=1)], axis=0)
    a1_scr[n_rows:, :] = gm.astype(BF16)

    c_re2 = rowp_ref[0:C, :]
    c_im2 = rowp_ref[C:2 * C, :]
    kr = jnp.dot(jnp.where(low, c_re2, -c_im2), gm, precision=lax.Precision.HIGHEST,
                 preferred_element_type=F32)
    krow = lax.broadcasted_iota(jnp.int32, (C, C * T), 0)
    klane = lax.broadcasted_iota(jnp.int32, (C, C * T), 1)
    kr = kr + jnp.where(klane == krow * T + (T - 1), dcol_ref[:, 0:1], 0.0)

    t_idx = lax.broadcasted_iota(jnp.int32, (T, LANES), 0)
    s_idx = lax.broadcasted_iota(jnp.int32, (T, LANES), 1) & (T - 1)
    causal = s_idx <= t_idx
    for c in range(C):
        for k in range(C * T // LANES):
            z = jnp.broadcast_to(kr[c:c + 1, k * LANES:(k + 1) * LANES], (T, LANES))
            z = pltpu.roll(z, LANES - (T - 1), axis=1, stride=1, stride_axis=0)
            a1_scr[c * T:(c + 1) * T, k * LANES:(k + 1) * LANES] = jnp.where(causal, z, 0.0).astype(BF16)

    dt_r = jnp.exp(rowp_ref[34:35, :])
    steps = (lax.broadcasted_iota(jnp.int32, (T, LANES), 0) + 1).astype(F32)
    pr, pi = _cexp(steps, dt_r * rowp_ref[32:33, :], dt_r * rowp_ref[33:34, :])
    for c in range(C):
        cr, ci = c_re2[c:c + 1, :], c_im2[c:c + 1, :]
        cw_scr[c * T:(c + 1) * T, :] = jnp.where(low, cr * pr - ci * pi, -(cr * pi + ci * pr)).astype(BF16)

    u = jnp.concatenate([u_ref[b].reshape(n_rows, chunks_per_seq) for b in range(n_batch)], axis=1)
    y1 = jnp.dot(a1_scr[...], u, preferred_element_type=F32)
    xr = y1[n_rows:n_rows + P]
    xi = y1[n_rows + P:]
    kidx = lax.broadcasted_iota(jnp.int32, (P, n_cols), 1) & (chunks_per_seq - 1)
    mr, mi = _cexp(float(T), lam_re, lam_im)
    sh = 1
    while sh < chunks_per_seq:
        rr = pltpu.roll(xr, sh, axis=1)
        ri = pltpu.roll(xi, sh, axis=1)
        ok = kidx >= sh
        xr, xi = (xr + jnp.where(ok, mr * rr - mi * ri, 0.0),
                  xi + jnp.where(ok, mr * ri + mi * rr, 0.0))
        mr, mi = mr * mr - mi * mi, 2.0 * mr * mi
        sh *= 2
    ok = kidx >= 1
    hr = jnp.where(ok, pltpu.roll(xr, 1, axis=1), 0.0)
    hi = jnp.where(ok, pltpu.roll(xi, 1, axis=1), 0.0)
    h = jnp.concatenate([hr, hi], axis=0).astype(BF16)
    y = y1[:n_rows] + jnp.dot(cw_scr[...], h, preferred_element_type=F32)
    z = jax.nn.gelu(y).astype(BF16)
    for b in range(n_batch):
        y_ref[b] = z[:, b * chunks_per_seq:(b + 1) * chunks_per_seq].reshape(C, T, chunks_per_seq)


def _ssm(ut, colp, rowp, dcol):
    batch, _, _, chunks_per_seq = ut.shape
    n_rows = SSM_GROUP * SSM_CHUNK
    per_step = SSM_GROUPS_PER_STEP
    blk = pl.BlockSpec((batch, per_step * SSM_GROUP, SSM_CHUNK, chunks_per_seq), lambda g: (0, g, 0, 0))
    per_group = lambda a: pl.BlockSpec((per_step,) + a.shape[1:], lambda g: (g, 0, 0))
    return pl.pallas_call(
        functools.partial(_ssm_kernel, chunks_per_seq=chunks_per_seq),
        grid=(N_SSM_GROUPS // per_step,),
        in_specs=[blk, per_group(colp), per_group(rowp), per_group(dcol)],
        out_specs=blk,
        out_shape=jax.ShapeDtypeStruct(ut.shape, BF16),
        scratch_shapes=[pltpu.VMEM((per_step, n_rows + 2 * STATE_DIM, n_rows), BF16),
                        pltpu.VMEM((per_step, n_rows, 2 * STATE_DIM), BF16)],
        compiler_params=pltpu.CompilerParams(dimension_semantics=("arbitrary",)),
        name="ssm",
    )(ut, colp, rowp, dcol)


def _pack_ssm_params(a_re, a_im, log_dt, b_re, b_im, c_re, c_im, d_skip):
    G, P, C = N_SSM_GROUPS, STATE_DIM, SSM_GROUP
    f = lambda t: t.astype(F32)
    log_dt_col = jnp.broadcast_to(f(log_dt)[:, None, None], (G, P, 1))
    colp = jnp.concatenate([f(a_re)[:, :, None], f(a_im)[:, :, None], f(b_re), f(b_im), log_dt_col,
                            jnp.zeros((G, P, LANES - 3 - 2 * C), F32)], axis=2)
    twice = lambda t: jnp.concatenate([f(t), f(t)], axis=-1)
    rowp = jnp.concatenate([twice(c_re), twice(c_im), twice(a_re)[:, None, :], twice(a_im)[:, None, :],
                            jnp.broadcast_to(f(log_dt)[:, None, None], (G, 1, 2 * P)),
                            jnp.zeros((G, 5, 2 * P), F32)], axis=1)
    dcol = jnp.broadcast_to(f(d_skip).reshape(G, C, 1), (G, C, LANES))
    return colp, rowp, dcol


def _mix_ffn2_kernel(x1_ref, o1_ref, o4_ref, o16_ref, l1_ref, l4_ref, l16_ref, z_ref,
                     wglu_ref, bglu_ref, wout_ref, gpost_ref, g3_ref, win_ref, wo_ref, p3_ref,
                     y_ref, o_scr, l_scr, x2_scr):
    tm = TOKEN_TILE
    step = pl.program_id(0)

    @pl.when(step == 0)
    def _init():
        x2_scr[...] = jnp.zeros(x2_scr.shape, F32)

    prev_slot = lax.rem(step + 1, 2)
    h3 = _rms(x2_scr[prev_slot], g3_ref[...]).astype(BF16)
    y_ref[...] = _swiglu_ffn(lambda: x2_scr[prev_slot], h3, win_ref, wo_ref, p3_ref[...])

    for n, (d, o_ref, l_ref) in enumerate(((4, o4_ref, l4_ref), (16, o16_ref, l16_ref))):
        for r in range(d):
            for c in range(ATTN_WIDTH // LANES):
                o_scr[n, c, pl.ds(r, tm // d, stride=d), :] = o_ref[0, r, :, c * LANES:(c + 1) * LANES].astype(F32)
            for c in range(STATS_WIDTH // LANES):
                l_scr[n, c, pl.ds(r, tm // d, stride=d), :] = l_ref[0, r, :, c * LANES:(c + 1) * LANES]
    ms = (l1_ref[...], l_scr[0, 0], l_scr[1, 0])
    ls = [pltpu.roll(st, LANES - L_LANE_OFFSET, axis=1) for st in ms]
    m = jnp.maximum(jnp.maximum(ms[0], ms[1]), ms[2])
    es = [jnp.exp2(mn - m) for mn in ms]
    inv = 1.0 / (es[0] * ls[0] + es[1] * ls[1] + es[2] * ls[2])
    ws = [e * inv for e in es]
    low_half = lax.broadcasted_iota(jnp.int32, (tm, 2 * HEAD_DIM), 1) < HEAD_DIM
    pairs = []
    for hp in range(N_HEADS // 2):
        cs = slice(2 * HEAD_DIM * hp, 2 * HEAD_DIM * (hp + 1))
        os_ = (o1_ref[:, cs].astype(F32), o_scr[0, hp], o_scr[1, hp])
        acc = jnp.zeros((tm, 2 * HEAD_DIM), F32)
        for w, o in zip(ws, os_):
            wexp = jnp.where(low_half, w[:, hp:hp + 1], w[:, HEAD_DIM + hp:HEAD_DIM + hp + 1])
            acc = acc + wexp * o
        pairs.append(acc)
    attn = jnp.concatenate(pairs, axis=-1).astype(BF16)
    z = z_ref[...]
    gate = jax.nn.sigmoid(jnp.dot(z, wglu_ref[...], preferred_element_type=F32) + bglu_ref[...])
    ssm = (z.astype(F32) * gate).astype(BF16)
    mixed = (jnp.dot(attn, wout_ref[0:ATTN_WIDTH, :], preferred_element_type=F32)
             + jnp.dot(ssm, wout_ref[ATTN_WIDTH:, :], preferred_element_type=F32))
    x2_scr[lax.rem(step, 2)] = x1_ref[...] + _rms(mixed, gpost_ref[...])


def _mix_ffn2(x1, o1, o4, o16, l1, l4, l16, z, wglu, bglu, wout, gpost, g3, win, wo, p3, seq_len):
    rows = x1.shape[0]
    tm = TOKEN_TILE
    n_tiles = rows // tm
    tok_now, res_now = _token_specs(n_tiles, seq_len // tm, 0)
    tok_prev, _ = _token_specs(n_tiles, seq_len // tm, 1)
    return pl.pallas_call(
        _mix_ffn2_kernel,
        grid=(n_tiles + 1,),
        in_specs=[tok_now(D_MODEL), tok_now(ATTN_WIDTH), res_now(4, ATTN_WIDTH), res_now(16, ATTN_WIDTH),
                  tok_now(STATS_WIDTH), res_now(4, STATS_WIDTH), res_now(16, STATS_WIDTH), tok_now(SSM_WIDTH),
                  _const_spec((SSM_WIDTH, SSM_WIDTH)), _const_spec((1, SSM_WIDTH)),
                  _const_spec((D_MODEL, D_MODEL)), _const_spec((1, D_MODEL)), _const_spec((1, D_MODEL)),
                  _const_spec((D_MODEL, 2 * D_FF)), _const_spec((D_FF, D_MODEL)),
                  _const_spec((1, D_MODEL))],
        out_specs=tok_prev(D_MODEL),
        out_shape=jax.ShapeDtypeStruct((rows, D_MODEL), F32),
        scratch_shapes=[pltpu.VMEM((2, ATTN_WIDTH // LANES, tm, LANES), F32),
                        pltpu.VMEM((2, STATS_WIDTH // LANES, tm, LANES), F32),
                        pltpu.VMEM((2, tm, D_MODEL), F32)],
        compiler_params=pltpu.CompilerParams(
            dimension_semantics=("arbitrary",), vmem_limit_bytes=VMEM_LIMIT_BYTES),
        name="mix_ffn2",
    )(x1, o1, o4, o16, l1, l4, l16, z, wglu, bglu, wout, gpost, g3, win, wo, p3)


def _row(v):
    return v.astype(F32).reshape(1, -1)


def kernel(x, ffn1_pre_g, ffn1_w_in, ffn1_w_out, ffn1_post_g, mix_pre_g, w_mix_in, a_re, a_im, log_dt, b_re, b_im, c_re, c_im, d_skip, w_glu, b_glu, w_mix_out, mix_post_g, ffn2_pre_g, ffn2_w_in, ffn2_w_out, ffn2_post_g):
    B, S, _ = x.shape
    depth = ffn1_pre_g.shape[0]
    n_chunks = S // SSM_CHUNK
    for l in range(depth):
        (x1, q1, k1, v1, q4, k4, v4, q16, k16, v16, u) = _ffn1_proj(
            x, _row(ffn1_pre_g[l]), ffn1_w_in[l].astype(BF16), ffn1_w_out[l].astype(BF16),
            _row(ffn1_post_g[l]), _row(mix_pre_g[l]), w_mix_in[l].astype(BF16))

        outs = []
        for d, (q, k, v) in zip(DILATIONS, ((q1, k1, v1), (q4, k4, v4), (q16, k16, v16))):
            flat = lambda t: t.reshape(B * S, ATTN_WIDTH)
            outs.append(_attention_branch(flat(q), flat(k), flat(v), S // d, d))
        (o1, l1), (o4, l4), (o16, l16) = outs
        o4 = o4.reshape(B, 4, S // 4, ATTN_WIDTH)
        l4 = l4.reshape(B, 4, S // 4, STATS_WIDTH)
        o16 = o16.reshape(B, 16, S // 16, ATTN_WIDTH)
        l16 = l16.reshape(B, 16, S // 16, STATS_WIDTH)

        colp, rowp, dcol = _pack_ssm_params(a_re[l], a_im[l], log_dt[l], b_re[l], b_im[l], c_re[l], c_im[l],
                                            d_skip[l])
        ut = u.reshape(B, n_chunks, SSM_CHUNK, SSM_WIDTH).transpose(0, 3, 2, 1)
        zt = _ssm(ut, colp, rowp, dcol)
        z = zt.transpose(0, 3, 2, 1).reshape(B * S, SSM_WIDTH)

        x = _mix_ffn2(x1, o1, o4, o16, l1, l4, l16, z, w_glu[l].astype(BF16), _row(b_glu[l]),
                      w_mix_out[l].astype(BF16), _row(mix_post_g[l]), _row(ffn2_pre_g[l]),
                      ffn2_w_in[l].astype(BF16), ffn2_w_out[l].astype(BF16), _row(ffn2_post_g[l]),
                      S).reshape(B, S, D_MODEL)
    return x
```

```python
import functools

import jax
import jax.numpy as jnp
from jax import lax
from jax.experimental import pallas as pl
from jax.experimental.pallas import tpu as pltpu

F32 = jnp.float32
BF16 = jnp.bfloat16

D_MODEL = 1024
ATTN_WIDTH = 512
SSM_WIDTH = 512
HEAD_DIM = 64
N_HEADS = 8
DILATIONS = (1, 4, 16)
WINDOW_STEPS = 128
QBLK = 128
SSM_GROUP = 16
N_SSM_GROUPS = 32
STATE_DIM = 64
D_FF = 2816
NORM_EPS = 1e-6

TOKEN_TILE = 512
FF_CHUNKS = (256,) * 11
SSM_CHUNK = 64
SSM_GROUPS_PER_STEP = 2
ATTN_TILE = 2048
MASK_VALUE = float("-inf")
LANES = 128
LOG2_E = 1.4426950408889634
QK_SCALE_LOG2 = HEAD_DIM ** -0.5 * LOG2_E
STATS_WIDTH = 2 * HEAD_DIM
L_LANE_OFFSET = 16
PROJ_SLABS = 4 * ATTN_WIDTH // LANES
VMEM_LIMIT_BYTES = 56 * 1024 * 1024


def _rms(x, g):
    return x * lax.rsqrt(jnp.mean(x * x, axis=-1, keepdims=True) + NORM_EPS) * g


def _swiglu_chunks(h, acc, chunks, wg_ref, wu_ref, wo_ref):
    for start, width in chunks:
        sl = slice(start, start + width)
        gate = jnp.dot(h, wg_ref[:, sl], preferred_element_type=F32)
        up = jnp.dot(h, wu_ref[:, sl], preferred_element_type=F32)
        act = (gate * jax.nn.sigmoid(gate) * up).astype(BF16)
        acc = acc + jnp.dot(act, wo_ref[sl, :], preferred_element_type=F32)
    return acc


def _ff_chunks():
    starts = [sum(FF_CHUNKS[:j]) for j in range(len(FF_CHUNKS))]
    return list(zip(starts, FF_CHUNKS))


def _swiglu_ffn(read_x, h, win_ref, wo_ref, post_g):
    wg_ref, wu_ref = win_ref.at[:, 0:D_FF], win_ref.at[:, D_FF:2 * D_FF]
    acc = _swiglu_chunks(h, jnp.zeros((h.shape[0], D_MODEL), F32), _ff_chunks(), wg_ref, wu_ref, wo_ref)
    return read_x() + 0.5 * _rms(acc, post_g)


def _const_spec(shape):
    return pl.BlockSpec(shape, lambda *_: (0,) * len(shape), pipeline_mode=pl.Buffered(1))


def _token_specs(n_tiles, tiles_per_seq, lag):
    tm = TOKEN_TILE

    def tile(i):
        return jnp.clip(i - lag, 0, n_tiles - 1)

    def tok(w):
        return pl.BlockSpec((tm, w), lambda i: (tile(i), 0))

    def res(d, w):
        return pl.BlockSpec((1, d, tm // d, w),
                            lambda i: (tile(i) // tiles_per_seq, 0, tile(i) % tiles_per_seq, 0))

    return tok, res


def _run_pipelined(lagging, leading):
    step = pl.program_id(0)
    last = pl.num_programs(0) - 1

    @pl.when(step == 0)
    def _first():
        leading()

    @pl.when(jnp.logical_and(step > 0, step < last))
    def _interior():
        lagging()
        leading()

    @pl.when(step == last)
    def _last():
        lagging()


def _ffn1_proj_kernel(x_ref, g1_ref, win_ref, wo_ref, p1_ref, gm_ref, wm_ref,
                      x1_ref, q1_ref, k1_ref, v1_ref, q4_ref, k4_ref, v4_ref,
                      q16_ref, k16_ref, v16_ref, u_ref, proj_scr, mod4_scr):
    lanes_per = ATTN_WIDTH // LANES

    def write_previous_tile():
        q4_rows = TOKEN_TILE // 4
        outs = ((q1_ref, q4_ref, q16_ref), (k1_ref, k4_ref, k16_ref), (v1_ref, v4_ref, v16_ref))
        for t, (o1, o4, o16) in enumerate(outs):
            for c in range(lanes_per):
                cs = slice(c * LANES, (c + 1) * LANES)
                slab = t * lanes_per + c
                o1[:, cs] = proj_scr[slab].astype(BF16)
                for b in range(4):
                    cls4 = proj_scr[slab, pl.ds(b, q4_rows, stride=4), :]
                    o4[0, b, :, cs] = cls4.astype(BF16)
                    mod4_scr[slab, b * q4_rows:(b + 1) * q4_rows, :] = cls4
                for b in range(4):
                    for a in range(4):
                        o16[0, 4 * a + b, :, cs] = mod4_scr[
                            slab, pl.ds(b * q4_rows + a, q4_rows // 4, stride=4), :].astype(BF16)
        for c in range(lanes_per):
            u_ref[:, c * LANES:(c + 1) * LANES] = proj_scr[3 * lanes_per + c].astype(BF16)

    def compute_tile():
        h1 = _rms(x_ref[...], g1_ref[...]).astype(BF16)
        x1 = _swiglu_ffn(lambda: x_ref[...], h1, win_ref, wo_ref, p1_ref[...])
        x1_ref[...] = x1
        h = _rms(x1, gm_ref[...]).astype(BF16)
        proj = jnp.dot(h, wm_ref[...], preferred_element_type=F32)
        for cb in range(PROJ_SLABS):
            slab = proj[:, cb * LANES:(cb + 1) * LANES]
            proj_scr[cb] = slab * QK_SCALE_LOG2 if cb < lanes_per else slab

    _run_pipelined(write_previous_tile, compute_tile)


def _ffn1_proj(x, g1, win, wo, p1, gm, wm):
    B, S, _ = x.shape
    tm = TOKEN_TILE
    n_tiles = B * S // tm
    tok_now, _ = _token_specs(n_tiles, S // tm, 0)
    tok_prev, res_prev = _token_specs(n_tiles, S // tm, 1)
    nat = jax.ShapeDtypeStruct((B * S, ATTN_WIDTH), BF16)
    r4 = jax.ShapeDtypeStruct((B, 4, S // 4, ATTN_WIDTH), BF16)
    r16 = jax.ShapeDtypeStruct((B, 16, S // 16, ATTN_WIDTH), BF16)
    return pl.pallas_call(
        _ffn1_proj_kernel,
        grid=(n_tiles + 1,),
        in_specs=[tok_now(D_MODEL), _const_spec((1, D_MODEL)),
                  _const_spec((D_MODEL, 2 * D_FF)), _const_spec((D_FF, D_MODEL)),
                  _const_spec((1, D_MODEL)), _const_spec((1, D_MODEL)), _const_spec((D_MODEL, 4 * ATTN_WIDTH))],
        out_specs=[tok_now(D_MODEL)] + [tok_prev(ATTN_WIDTH)] * 3 + [res_prev(4, ATTN_WIDTH)] * 3
        + [res_prev(16, ATTN_WIDTH)] * 3 + [tok_prev(SSM_WIDTH)],
        out_shape=[jax.ShapeDtypeStruct((B * S, D_MODEL), F32)] + [nat] * 3 + [r4] * 3 + [r16] * 3 + [nat],
        scratch_shapes=[pltpu.VMEM((PROJ_SLABS, tm, LANES), F32),
                        pltpu.VMEM((3 * ATTN_WIDTH // LANES, tm, LANES), F32)],
        compiler_params=pltpu.CompilerParams(
            dimension_semantics=("arbitrary",), vmem_limit_bytes=VMEM_LIMIT_BYTES),
        name="ffn1_proj",
    )(x.reshape(B * S, D_MODEL), g1, win, wo, p1, gm, wm)


def _attn_kernel(q_ref, kc_ref, kp_ref, vc_ref, vp_ref, o_ref, st_ref, bias_ref, *, blocks_per_seq, dilation):
    @pl.when(pl.program_id(0) == 0)
    def _build_bias_tables():
        qi = lax.broadcasted_iota(jnp.int32, (QBLK, 2 * QBLK), 0)
        ci = lax.broadcasted_iota(jnp.int32, (QBLK, 2 * QBLK), 1)
        steps = QBLK + qi - ci
        dist = (steps * dilation).astype(F32)
        band = jnp.where(steps >= 0, jnp.where(steps <= WINDOW_STEPS, 1, 0), 0)
        band_first = jnp.where(ci >= QBLK, band, 0)
        for h in range(N_HEADS):
            bias = -(2.0 ** (-8.0 * (h + 1) / N_HEADS) * LOG2_E) * dist
            bias_ref[0, h] = jnp.where(band == 1, bias, MASK_VALUE)
            bias_ref[1, h] = jnp.where(band_first == 1, bias, MASK_VALUE)

    n_blocks = ATTN_TILE // QBLK
    lane = lax.broadcasted_iota(jnp.int32, (1, 2 * HEAD_DIM), 1)
    head_mask = ((lane < HEAD_DIM).astype(BF16), (lane >= HEAD_DIM).astype(BF16))
    ones_cols = tuple(jnp.broadcast_to(mk, (2 * QBLK, 2 * HEAD_DIM)) for mk in head_mask)
    lane_f = lax.broadcasted_iota(jnp.int32, (QBLK, 2 * HEAD_DIM), 1)
    low_half = lane_f < HEAD_DIM

    for j in range(n_blocks):
        rows = slice(j * QBLK, (j + 1) * QBLK)
        if blocks_per_seq >= n_blocks:
            tiles_per_seq = blocks_per_seq // n_blocks
            table = (lax.rem(pl.program_id(0), tiles_per_seq) == 0).astype(jnp.int32) if j == 0 else 0
        else:
            table = 1 if j % blocks_per_seq == 0 else 0
        st_tile = jnp.ones((QBLK, 2 * HEAD_DIM), F32)
        for hp in range(N_HEADS // 2):
            cs = slice(2 * HEAD_DIM * hp, 2 * HEAD_DIM * (hp + 1))
            q = q_ref[rows, cs]
            if j == 0:
                kk = jnp.concatenate([kp_ref[:, cs], kc_ref[0:QBLK, cs]], axis=0)
                vv = jnp.concatenate([vp_ref[:, cs], vc_ref[0:QBLK, cs]], axis=0)
            else:
                kk = kc_ref[(j - 1) * QBLK:(j + 1) * QBLK, cs]
                vv = vc_ref[(j - 1) * QBLK:(j + 1) * QBLK, cs]
            ps, ms, vas = [], [], []
            for e in range(2):
                s = lax.dot_general(q * head_mask[e], kk, (((1,), (1,)), ((), ())),
                                    preferred_element_type=F32)
                s = s + bias_ref[table, 2 * hp + e]
                m = jnp.max(s, axis=-1, keepdims=True)
                ps.append(jnp.exp2(s - m).astype(BF16))
                ms.append(m)
                vas.append(jnp.concatenate([vv * head_mask[e], ones_cols[e]], axis=1))
            pv = jnp.dot(jnp.concatenate(ps, axis=1), jnp.concatenate(vas, axis=0),
                         preferred_element_type=F32)
            o_ref[rows, cs] = pv[:, :2 * HEAD_DIM].astype(BF16)
            in_half = lane_f & (HEAD_DIM - 1)
            st_tile = jnp.where(in_half == hp, jnp.where(low_half, ms[0], ms[1]),
                                jnp.where(in_half == hp + L_LANE_OFFSET, pv[:, 2 * HEAD_DIM:], st_tile))
        st_ref[rows, :] = st_tile


def _attention_branch(q, k, v, seq_len, dilation):
    rows = q.shape[0]
    ratio = ATTN_TILE // QBLK
    cur = pl.BlockSpec((ATTN_TILE, ATTN_WIDTH), lambda i: (i, 0))
    prev = pl.BlockSpec((QBLK, ATTN_WIDTH), lambda i: (jnp.maximum(i * ratio - 1, 0), 0))
    return pl.pallas_call(
        functools.partial(_attn_kernel, blocks_per_seq=seq_len // QBLK, dilation=dilation),
        grid=(rows // ATTN_TILE,),
        in_specs=[cur, cur, prev, cur, prev],
        out_specs=[cur, pl.BlockSpec((ATTN_TILE, STATS_WIDTH), lambda i: (i, 0))],
        out_shape=[jax.ShapeDtypeStruct((rows, ATTN_WIDTH), BF16),
                   jax.ShapeDtypeStruct((rows, STATS_WIDTH), F32)],
        scratch_shapes=[pltpu.VMEM((2, N_HEADS, QBLK, 2 * QBLK), F32)],
        compiler_params=pltpu.CompilerParams(dimension_semantics=("arbitrary",)),
        name=f"attn_s{seq_len}",
    )(q, k, k, v, v)


def _cexp(n, lam_re, lam_im):
    mag = jnp.exp(n * lam_re)
    return mag * jnp.cos(n * lam_im), mag * jnp.sin(n * lam_im)


def _ssm_kernel(u_ref, colp_ref, rowp_ref, dcol_ref, y_ref, a1_scr, cw_scr, *, chunks_per_seq):
    for gi in range(SSM_GROUPS_PER_STEP):
        rows = pl.ds(gi * SSM_GROUP, SSM_GROUP)
        _ssm_group(u_ref.at[:, rows], colp_ref.at[gi], rowp_ref.at[gi], dcol_ref.at[gi], y_ref.at[:, rows],
                   a1_scr.at[gi], cw_scr.at[gi], chunks_per_seq)


def _ssm_group(u_ref, colp_ref, rowp_ref, dcol_ref, y_ref, a1_scr, cw_scr, chunks_per_seq):
    T, P, C = SSM_CHUNK, STATE_DIM, SSM_GROUP
    n_rows = C * T
    n_batch = u_ref.shape[0]
    n_cols = n_batch * chunks_per_seq
    lane = lax.broadcasted_iota(jnp.int32, (1, LANES), 1)
    low = lane < P

    a_re = colp_ref[:, 0:1]
    a_im = colp_ref[:, 1:2]
    dt = jnp.exp(colp_ref[:, 34:35])
    lam_re, lam_im = dt * a_re, dt * a_im
    ab_re, ab_im = _cexp(1.0, lam_re, lam_im)
    inv_a2 = 1.0 / (a_re * a_re + a_im * a_im)
    nr, ni = ab_re - 1.0, ab_im
    cf_re = (nr * a_re + ni * a_im) * inv_a2
    cf_im = (ni * a_re - nr * a_im) * inv_a2

    rev = (T - 1 - (lane & (T - 1))).astype(F32)
    pw_re, pw_im = _cexp(rev, lam_re, lam_im)
    g_re, g_im = [], []
    for k in range(C // 2):
        b_re = jnp.where(low, colp_ref[:, 2 + 2 * k:3 + 2 * k], colp_ref[:, 3 + 2 * k:4 + 2 * k])
        b_im = jnp.where(low, colp_ref[:, 18 + 2 * k:19 + 2 * k], colp_ref[:, 19 + 2 * k:20 + 2 * k])
        bb_re = cf_re * b_re - cf_im * b_im
        bb_im = cf_re * b_im + cf_im * b_re
        g_re.append(pw_re * bb_re - pw_im * bb_im)
        g_im.append(pw_re * bb_im + pw_im * bb_re)
    gm = jnp.concatenate([jnp.concatenate(g_re, axis=1), jnp.concatenate(g_im, axis=1)], axis=0)
    a1_scr[n_rows:, :] = gm.astype(BF16)

    c_re2 = rowp_ref[0:C, :]
    c_im2 = rowp_ref[C:2 * C, :]
    kr = jnp.dot(jnp.where(low, c_re2, -c_im2), gm, precision=lax.Precision.HIGHEST,
                 preferred_element_type=F32)
    krow = lax.broadcasted_iota(jnp.int32, (C, C * T), 0)
    klane = lax.broadcasted_iota(jnp.int32, (C, C * T), 1)
    kr = kr + jnp.where(klane == krow * T + (T - 1), dcol_ref[:, 0:1], 0.0)

    t_idx = lax.broadcasted_iota(jnp.int32, (T, LANES), 0)
    s_idx = lax.broadcasted_iota(jnp.int32, (T, LANES), 1) & (T - 1)
    causal = s_idx <= t_idx
    for c in range(C):
        for k in range(C * T // LANES):
            z = jnp.broadcast_to(kr[c:c + 1, k * LANES:(k + 1) * LANES], (T, LANES))
            z = pltpu.roll(z, LANES - (T - 1), axis=1, stride=1, stride_axis=0)
            a1_scr[c * T:(c + 1) * T, k * LANES:(k + 1) * LANES] = jnp.where(causal, z, 0.0).astype(BF16)

    dt_r = jnp.exp(rowp_ref[34:35, :])
    steps = (lax.broadcasted_iota(jnp.int32, (T, LANES), 0) + 1).astype(F32)
    pr, pi = _cexp(steps, dt_r * rowp_ref[32:33, :], dt_r * rowp_ref[33:34, :])
    for c in range(C):
        cr, ci = c_re2[c:c + 1, :], c_im2[c:c + 1, :]
        cw_scr[c * T:(c + 1) * T, :] = jnp.where(low, cr * pr - ci * pi, -(cr * pi + ci * pr)).astype(BF16)

    u = jnp.concatenate([u_ref[b].reshape(n_rows, chunks_per_seq) for b in range(n_batch)], axis=1)
    y1 = jnp.dot(a1_scr[...], u, preferred_element_type=F32)
    xr = y1[n_rows:n_rows + P]
    xi = y1[n_rows + P:]
    kidx = lax.broadcasted_iota(jnp.int32, (P, n_cols), 1) & (chunks_per_seq - 1)
    mr, mi = _cexp(float(T), lam_re, lam_im)
    sh = 1
    while sh < chunks_per_seq:
        rr = pltpu.roll(xr, sh, axis=1)
        ri = pltpu.roll(xi, sh, axis=1)
        ok = kidx >= sh
        xr, xi = (xr + jnp.where(ok, mr * rr - mi * ri, 0.0),
                  xi + jnp.where(ok, mr * ri + mi * rr, 0.0))
        mr, mi = mr * mr - mi * mi, 2.0 * mr * mi
        sh *= 2
    ok = kidx >= 1
    hr = jnp.where(ok, pltpu.roll(xr, 1, axis=1), 0.0)
    hi = jnp.where(ok, pltpu.roll(xi, 1, axis=1), 0.0)
    h = jnp.concatenate([hr, hi], axis=0).astype(BF16)
    y = y1[:n_rows] + jnp.dot(cw_scr[...], h, preferred_element_type=F32)
    z = jax.nn.gelu(y).astype(BF16)
    for b in range(n_batch):
        y_ref[b] = z[:, b * chunks_per_seq:(b + 1) * chunks_per_seq].reshape(C, T, chunks_per_seq)


def _ssm(ut, colp, rowp, dcol):
    batch, _, _, chunks_per_seq = ut.shape
    n_rows = SSM_GROUP * SSM_CHUNK
    per_step = SSM_GROUPS_PER_STEP
    blk = pl.BlockSpec((batch, per_step * SSM_GROUP, SSM_CHUNK, chunks_per_seq), lambda g: (0, g, 0, 0))
    per_group = lambda a: pl.BlockSpec((per_step,) + a.shape[1:], lambda g: (g, 0, 0))
    return pl.pallas_call(
        functools.partial(_ssm_kernel, chunks_per_seq=chunks_per_seq),
        grid=(N_SSM_GROUPS // per_step,),
        in_specs=[blk, per_group(colp), per_group(rowp), per_group(dcol)],
        out_specs=blk,
        out_shape=jax.ShapeDtypeStruct(ut.shape, BF16),
        scratch_shapes=[pltpu.VMEM((per_step, n_rows + 2 * STATE_DIM, n_rows), BF16),
                        pltpu.VMEM((per_step, n_rows, 2 * STATE_DIM), BF16)],
        compiler_params=pltpu.CompilerParams(dimension_semantics=("arbitrary",)),
        name="ssm",
    )(ut, colp, rowp, dcol)


def _pack_ssm_params(a_re, a_im, log_dt, b_re, b_im, c_re, c_im, d_skip):
    G, P, C = N_SSM_GROUPS, STATE_DIM, SSM_GROUP
    f = lambda t: t.astype(F32)
    log_dt_col = jnp.broadcast_to(f(log_dt)[:, None, None], (G, P, 1))
    colp = jnp.concatenate([f(a_re)[:, :, None], f(a_im)[:, :, None], f(b_re), f(b_im), log_dt_col,
                            jnp.zeros((G, P, LANES - 3 - 2 * C), F32)], axis=2)
    twice = lambda t: jnp.concatenate([f(t), f(t)], axis=-1)
    rowp = jnp.concatenate([twice(c_re), twice(c_im), twice(a_re)[:, None, :], twice(a_im)[:, None, :],
                            jnp.broadcast_to(f(log_dt)[:, None, None], (G, 1, 2 * P)),
                            jnp.zeros((G, 5, 2 * P), F32)], axis=1)
    dcol = jnp.broadcast_to(f(d_skip).reshape(G, C, 1), (G, C, LANES))
    return colp, rowp, dcol


def _mix_ffn2_kernel(x1_ref, o1_ref, o4_ref, o16_ref, l1_ref, l4_ref, l16_ref, z_ref,
                     wglu_ref, bglu_ref, wout_ref, gpost_ref, g3_ref, win_ref, wo_ref, p3_ref,
                     y_ref, o_scr, l_scr, x2_scr):
    tm = TOKEN_TILE
    step = pl.program_id(0)

    def ffn_previous_tile():
        prev_slot = lax.rem(step + 1, 2)
        h3 = _rms(x2_scr[prev_slot], g3_ref[...]).astype(BF16)
        y_ref[...] = _swiglu_ffn(lambda: x2_scr[prev_slot], h3, win_ref, wo_ref, p3_ref[...])

    def mix_tile():
        for n, (d, o_ref, l_ref) in enumerate(((4, o4_ref, l4_ref), (16, o16_ref, l16_ref))):
            for r in range(d):
                for c in range(ATTN_WIDTH // LANES):
                    o_scr[n, c, pl.ds(r, tm // d, stride=d), :] = (
                        o_ref[0, r, :, c * LANES:(c + 1) * LANES].astype(F32))
                for c in range(STATS_WIDTH // LANES):
                    l_scr[n, c, pl.ds(r, tm // d, stride=d), :] = l_ref[0, r, :, c * LANES:(c + 1) * LANES]
        ms = (l1_ref[...], l_scr[0, 0], l_scr[1, 0])
        ls = [pltpu.roll(st, LANES - L_LANE_OFFSET, axis=1) for st in ms]
        m = jnp.maximum(jnp.maximum(ms[0], ms[1]), ms[2])
        es = [jnp.exp2(mn - m) for mn in ms]
        inv = 1.0 / (es[0] * ls[0] + es[1] * ls[1] + es[2] * ls[2])
        ws = [e * inv for e in es]
        low_half = lax.broadcasted_iota(jnp.int32, (tm, 2 * HEAD_DIM), 1) < HEAD_DIM
        pairs = []
        for hp in range(N_HEADS // 2):
            cs = slice(2 * HEAD_DIM * hp, 2 * HEAD_DIM * (hp + 1))
            os_ = (o1_ref[:, cs].astype(F32), o_scr[0, hp], o_scr[1, hp])
            acc = jnp.zeros((tm, 2 * HEAD_DIM), F32)
            for w, o in zip(ws, os_):
                wexp = jnp.where(low_half, w[:, hp:hp + 1], w[:, HEAD_DIM + hp:HEAD_DIM + hp + 1])
                acc = acc + wexp * o
            pairs.append(acc)
        attn = jnp.concatenate(pairs, axis=-1).astype(BF16)
        z = z_ref[...]
        gate = jax.nn.sigmoid(jnp.dot(z, wglu_ref[...], preferred_element_type=F32) + bglu_ref[...])
        ssm = (z.astype(F32) * gate).astype(BF16)
        mixed = (jnp.dot(attn, wout_ref[0:ATTN_WIDTH, :], preferred_element_type=F32)
                 + jnp.dot(ssm, wout_ref[ATTN_WIDTH:, :], preferred_element_type=F32))
        x2_scr[lax.rem(step, 2)] = x1_ref[...] + _rms(mixed, gpost_ref[...])

    _run_pipelined(ffn_previous_tile, mix_tile)


def _mix_ffn2(x1, o1, o4, o16, l1, l4, l16, z, wglu, bglu, wout, gpost, g3, win, wo, p3, seq_len):
    rows = x1.shape[0]
    tm = TOKEN_TILE
    n_tiles = rows // tm
    tok_now, res_now = _token_specs(n_tiles, seq_len // tm, 0)
    tok_prev, _ = _token_specs(n_tiles, seq_len // tm, 1)
    return pl.pallas_call(
        _mix_ffn2_kernel,
        grid=(n_tiles + 1,),
        in_specs=[tok_now(D_MODEL), tok_now(ATTN_WIDTH), res_now(4, ATTN_WIDTH), res_now(16, ATTN_WIDTH),
                  tok_now(STATS_WIDTH), res_now(4, STATS_WIDTH), res_now(16, STATS_WIDTH), tok_now(SSM_WIDTH),
                  _const_spec((SSM_WIDTH, SSM_WIDTH)), _const_spec((1, SSM_WIDTH)),
                  _const_spec((D_MODEL, D_MODEL)), _const_spec((1, D_MODEL)), _const_spec((1, D_MODEL)),
                  _const_spec((D_MODEL, 2 * D_FF)), _const_spec((D_FF, D_MODEL)),
                  _const_spec((1, D_MODEL))],
        out_specs=tok_prev(D_MODEL),
        out_shape=jax.ShapeDtypeStruct((rows, D_MODEL), F32),
        scratch_shapes=[pltpu.VMEM((2, ATTN_WIDTH // LANES, tm, LANES), F32),
                        pltpu.VMEM((2, STATS_WIDTH // LANES, tm, LANES), F32),
                        pltpu.VMEM((2, tm, D_MODEL), F32)],
        compiler_params=pltpu.CompilerParams(
            dimension_semantics=("arbitrary",), vmem_limit_bytes=VMEM_LIMIT_BYTES),
        name="mix_ffn2",
    )(x1, o1, o4, o16, l1, l4, l16, z, wglu, bglu, wout, gpost, g3, win, wo, p3)


def _row(v):
    return v.astype(F32).reshape(1, -1)


def kernel(x, ffn1_pre_g, ffn1_w_in, ffn1_w_out, ffn1_post_g, mix_pre_g, w_mix_in, a_re, a_im, log_dt, b_re, b_im, c_re, c_im, d_skip, w_glu, b_glu, w_mix_out, mix_post_g, ffn2_pre_g, ffn2_w_in, ffn2_w_out, ffn2_post_g):
    B, S, _ = x.shape
    depth = ffn1_pre_g.shape[0]
    n_chunks = S // SSM_CHUNK
    for l in range(depth):
        (x1, q1, k1, v1, q4, k4, v4, q16, k16, v16, u) = _ffn1_proj(
            x, _row(ffn1_pre_g[l]), ffn1_w_in[l].astype(BF16), ffn1_w_out[l].astype(BF16),
            _row(ffn1_post_g[l]), _row(mix_pre_g[l]), w_mix_in[l].astype(BF16))

        outs = []
        for d, (q, k, v) in zip(DILATIONS, ((q1, k1, v1), (q4, k4, v4), (q16, k16, v16))):
            flat = lambda t: t.reshape(B * S, ATTN_WIDTH)
            outs.append(_attention_branch(flat(q), flat(k), flat(v), S // d, d))
        (o1, l1), (o4, l4), (o16, l16) = outs
        o4 = o4.reshape(B, 4, S // 4, ATTN_WIDTH)
        l4 = l4.reshape(B, 4, S // 4, STATS_WIDTH)
        o16 = o16.reshape(B, 16, S // 16, ATTN_WIDTH)
        l16 = l16.reshape(B, 16, S // 16, STATS_WIDTH)

        colp, rowp, dcol = _pack_ssm_params(a_re[l], a_im[l], log_dt[l], b_re[l], b_im[l], c_re[l], c_im[l],
                                            d_skip[l])
        ut = u.reshape(B, n_chunks, SSM_CHUNK, SSM_WIDTH).transpose(0, 3, 2, 1)
        zt = _ssm(ut, colp, rowp, dcol)
        z = zt.transpose(0, 3, 2, 1).reshape(B * S, SSM_WIDTH)

        x = _mix_ffn2(x1, o1, o4, o16, l1, l4, l16, z, w_glu[l].astype(BF16), _row(b_glu[l]),
                      w_mix_out[l].astype(BF16), _row(mix_post_g[l]), _row(ffn2_pre_g[l]),
                      ffn2_w_in[l].astype(BF16), ffn2_w_out[l].astype(BF16), _row(ffn2_post_g[l]),
                      S).reshape(B, S, D_MODEL)
    return x
```

```python
import functools

import jax
import jax.numpy as jnp
from jax import lax
from jax.experimental import pallas as pl
from jax.experimental.pallas import tpu as pltpu

F32 = jnp.float32
BF16 = jnp.bfloat16

D_MODEL = 1024
ATTN_WIDTH = 512
SSM_WIDTH = 512
HEAD_DIM = 64
N_HEADS = 8
DILATIONS = (1, 4, 16)
WINDOW_STEPS = 128
QBLK = 128
SSM_GROUP = 16
N_SSM_GROUPS = 32
STATE_DIM = 64
D_FF = 2816
NORM_EPS = 1e-6

TOKEN_TILE = 512
FF_CHUNKS = (256,) * 11
SSM_CHUNK = 64
SSM_GROUPS_PER_STEP = 2
ATTN_TILE = 2048
MASK_VALUE = float("-inf")
LANES = 128
LOG2_E = 1.4426950408889634
QK_SCALE_LOG2 = HEAD_DIM ** -0.5 * LOG2_E
STATS_WIDTH = 2 * HEAD_DIM
L_LANE_OFFSET = 16
PROJ_SLABS = 4 * ATTN_WIDTH // LANES
VMEM_LIMIT_BYTES = 56 * 1024 * 1024


def _rms(x, g):
    return x * lax.rsqrt(jnp.mean(x * x, axis=-1, keepdims=True) + NORM_EPS) * g


def _swiglu_chunks(h, acc, chunks, wg_ref, wu_ref, wo_ref):
    for start, width in chunks:
        sl = slice(start, start + width)
        gate = jnp.dot(h, wg_ref[:, sl], preferred_element_type=F32)
        up = jnp.dot(h, wu_ref[:, sl], preferred_element_type=F32)
        act = (gate * jax.nn.sigmoid(gate) * up).astype(BF16)
        acc = acc + jnp.dot(act, wo_ref[sl, :], preferred_element_type=F32)
    return acc


def _ff_chunks():
    starts = [sum(FF_CHUNKS[:j]) for j in range(len(FF_CHUNKS))]
    return list(zip(starts, FF_CHUNKS))


def _swiglu_ffn(read_x, h, win_ref, wo_ref, post_g):
    wg_ref, wu_ref = win_ref.at[:, 0:D_FF], win_ref.at[:, D_FF:2 * D_FF]
    acc = _swiglu_chunks(h, jnp.zeros((h.shape[0], D_MODEL), F32), _ff_chunks(), wg_ref, wu_ref, wo_ref)
    return read_x() + 0.5 * _rms(acc, post_g)


def _const_spec(shape):
    return pl.BlockSpec(shape, lambda *_: (0,) * len(shape), pipeline_mode=pl.Buffered(1))


def _token_specs(n_tiles, tiles_per_seq, lag):
    tm = TOKEN_TILE

    def tile(i):
        return jnp.clip(i - lag, 0, n_tiles - 1)

    def tok(w):
        return pl.BlockSpec((tm, w), lambda i: (tile(i), 0))

    def res(d, w):
        return pl.BlockSpec((1, d, tm // d, w),
                            lambda i: (tile(i) // tiles_per_seq, 0, tile(i) % tiles_per_seq, 0))

    return tok, res


def _run_pipelined(lagging, leading):
    step = pl.program_id(0)
    last = pl.num_programs(0) - 1

    @pl.when(step == 0)
    def _first():
        leading()

    @pl.when(jnp.logical_and(step > 0, step < last))
    def _interior():
        lagging()
        leading()

    @pl.when(step == last)
    def _last():
        lagging()


def _ffn1_proj_kernel(x_ref, g1_ref, win_ref, wo_ref, p1_ref, gm_ref, wm_ref,
                      x1_ref, q1_ref, k1_ref, v1_ref, q4_ref, k4_ref, v4_ref,
                      q16_ref, k16_ref, v16_ref, u_ref, proj_scr, mod4_scr):
    lanes_per = ATTN_WIDTH // LANES

    def write_previous_tile():
        q4_rows = TOKEN_TILE // 4
        outs = ((q1_ref, q4_ref, q16_ref), (k1_ref, k4_ref, k16_ref), (v1_ref, v4_ref, v16_ref))
        for t, (o1, o4, o16) in enumerate(outs):
            for c in range(lanes_per):
                cs = slice(c * LANES, (c + 1) * LANES)
                slab = t * lanes_per + c
                o1[:, cs] = proj_scr[slab].astype(BF16)
                for b in range(4):
                    cls4 = proj_scr[slab, pl.ds(b, q4_rows, stride=4), :]
                    o4[0, b, :, cs] = cls4.astype(BF16)
                    mod4_scr[slab, b * q4_rows:(b + 1) * q4_rows, :] = cls4
                for b in range(4):
                    for a in range(4):
                        o16[0, 4 * a + b, :, cs] = mod4_scr[
                            slab, pl.ds(b * q4_rows + a, q4_rows // 4, stride=4), :].astype(BF16)
        for c in range(lanes_per):
            u_ref[:, c * LANES:(c + 1) * LANES] = proj_scr[3 * lanes_per + c].astype(BF16)

    def compute_tile():
        h1 = _rms(x_ref[...], g1_ref[...]).astype(BF16)
        x1 = _swiglu_ffn(lambda: x_ref[...], h1, win_ref, wo_ref, p1_ref[...])
        x1_ref[...] = x1
        h = _rms(x1, gm_ref[...]).astype(BF16)
        proj = jnp.dot(h, wm_ref[...], preferred_element_type=F32)
        for cb in range(PROJ_SLABS):
            slab = proj[:, cb * LANES:(cb + 1) * LANES]
            proj_scr[cb] = slab * QK_SCALE_LOG2 if cb < lanes_per else slab

    _run_pipelined(write_previous_tile, compute_tile)


def _ffn1_proj(x, g1, win, wo, p1, gm, wm):
    B, S, _ = x.shape
    tm = TOKEN_TILE
    n_tiles = B * S // tm
    tok_now, _ = _token_specs(n_tiles, S // tm, 0)
    tok_prev, res_prev = _token_specs(n_tiles, S // tm, 1)
    nat = jax.ShapeDtypeStruct((B * S, ATTN_WIDTH), BF16)
    r4 = jax.ShapeDtypeStruct((B, 4, S // 4, ATTN_WIDTH), BF16)
    r16 = jax.ShapeDtypeStruct((B, 16, S // 16, ATTN_WIDTH), BF16)
    return pl.pallas_call(
        _ffn1_proj_kernel,
        grid=(n_tiles + 1,),
        in_specs=[tok_now(D_MODEL), _const_spec((1, D_MODEL)),
                  _const_spec((D_MODEL, 2 * D_FF)), _const_spec((D_FF, D_MODEL)),
                  _const_spec((1, D_MODEL)), _const_spec((1, D_MODEL)), _const_spec((D_MODEL, 4 * ATTN_WIDTH))],
        out_specs=[tok_now(D_MODEL)] + [tok_prev(ATTN_WIDTH)] * 3 + [res_prev(4, ATTN_WIDTH)] * 3
        + [res_prev(16, ATTN_WIDTH)] * 3 + [tok_prev(SSM_WIDTH)],
        out_shape=[jax.ShapeDtypeStruct((B * S, D_MODEL), F32)] + [nat] * 3 + [r4] * 3 + [r16] * 3 + [nat],
        scratch_shapes=[pltpu.VMEM((PROJ_SLABS, tm, LANES), F32),
                        pltpu.VMEM((3 * ATTN_WIDTH // LANES, tm, LANES), F32)],
        compiler_params=pltpu.CompilerParams(
            dimension_semantics=("arbitrary",), vmem_limit_bytes=VMEM_LIMIT_BYTES),
        name="ffn1_proj",
    )(x.reshape(B * S, D_MODEL), g1, win, wo, p1, gm, wm)


def _attn_kernel(q_ref, kc_ref, kp_ref, vc_ref, vp_ref, o_ref, st_ref, bias_ref, *, blocks_per_seq, dilation):
    @pl.when(pl.program_id(0) == 0)
    def _build_bias_tables():
        qi = lax.broadcasted_iota(jnp.int32, (QBLK, 2 * QBLK), 0)
        ci = lax.broadcasted_iota(jnp.int32, (QBLK, 2 * QBLK), 1)
        steps = QBLK + qi - ci
        dist = (steps * dilation).astype(F32)
        band = jnp.where(steps >= 0, jnp.where(steps <= WINDOW_STEPS, 1, 0), 0)
        band_first = jnp.where(ci >= QBLK, band, 0)
        for h in range(N_HEADS):
            bias = -(2.0 ** (-8.0 * (h + 1) / N_HEADS) * LOG2_E) * dist
            bias_ref[0, h] = jnp.where(band == 1, bias, MASK_VALUE)
            bias_ref[1, h] = jnp.where(band_first == 1, bias, MASK_VALUE)

    n_blocks = ATTN_TILE // QBLK
    lane = lax.broadcasted_iota(jnp.int32, (1, 2 * HEAD_DIM), 1)
    head_mask = ((lane < HEAD_DIM).astype(BF16), (lane >= HEAD_DIM).astype(BF16))
    ones_cols = tuple(jnp.broadcast_to(mk, (2 * QBLK, 2 * HEAD_DIM)) for mk in head_mask)
    lane_f = lax.broadcasted_iota(jnp.int32, (QBLK, 2 * HEAD_DIM), 1)
    low_half = lane_f < HEAD_DIM

    for j in range(n_blocks):
        rows = slice(j * QBLK, (j + 1) * QBLK)
        if blocks_per_seq >= n_blocks:
            tiles_per_seq = blocks_per_seq // n_blocks
            table = (lax.rem(pl.program_id(0), tiles_per_seq) == 0).astype(jnp.int32) if j == 0 else 0
        else:
            table = 1 if j % blocks_per_seq == 0 else 0
        st_tile = jnp.ones((QBLK, 2 * HEAD_DIM), F32)
        for hp in range(N_HEADS // 2):
            cs = slice(2 * HEAD_DIM * hp, 2 * HEAD_DIM * (hp + 1))
            q = q_ref[rows, cs]
            if j == 0:
                kk = jnp.concatenate([kp_ref[:, cs], kc_ref[0:QBLK, cs]], axis=0)
                vv = jnp.concatenate([vp_ref[:, cs], vc_ref[0:QBLK, cs]], axis=0)
            else:
                kk = kc_ref[(j - 1) * QBLK:(j + 1) * QBLK, cs]
                vv = vc_ref[(j - 1) * QBLK:(j + 1) * QBLK, cs]
            ps, ms, vas = [], [], []
            for e in range(2):
                s = lax.dot_general(q * head_mask[e], kk, (((1,), (1,)), ((), ())),
                                    preferred_element_type=F32)
                s = s + bias_ref[table, 2 * hp + e]
                m = jnp.max(s, axis=-1, keepdims=True)
                ps.append(jnp.exp2(s - m).astype(BF16))
                ms.append(m)
                vas.append(jnp.concatenate([vv * head_mask[e], ones_cols[e]], axis=1))
            pv = jnp.dot(jnp.concatenate(ps, axis=1), jnp.concatenate(vas, axis=0),
                         preferred_element_type=F32)
            o_ref[rows, cs] = pv[:, :2 * HEAD_DIM].astype(BF16)
            in_half = lane_f & (HEAD_DIM - 1)
            st_tile = jnp.where(in_half == hp, jnp.where(low_half, ms[0], ms[1]),
                                jnp.where(in_half == hp + L_LANE_OFFSET, pv[:, 2 * HEAD_DIM:], st_tile))
        st_ref[rows, :] = st_tile


def _attention_branch(q, k, v, seq_len, dilation):
    rows = q.shape[0]
    ratio = ATTN_TILE // QBLK
    cur = pl.BlockSpec((ATTN_TILE, ATTN_WIDTH), lambda i: (i, 0))
    prev = pl.BlockSpec((QBLK, ATTN_WIDTH), lambda i: (jnp.maximum(i * ratio - 1, 0), 0))
    return pl.pallas_call(
        functools.partial(_attn_kernel, blocks_per_seq=seq_len // QBLK, dilation=dilation),
        grid=(rows // ATTN_TILE,),
        in_specs=[cur, cur, prev, cur, prev],
        out_specs=[cur, pl.BlockSpec((ATTN_TILE, STATS_WIDTH), lambda i: (i, 0))],
        out_shape=[jax.ShapeDtypeStruct((rows, ATTN_WIDTH), BF16),
                   jax.ShapeDtypeStruct((rows, STATS_WIDTH), F32)],
        scratch_shapes=[pltpu.VMEM((2, N_HEADS, QBLK, 2 * QBLK), F32)],
        compiler_params=pltpu.CompilerParams(dimension_semantics=("arbitrary",)),
        name=f"attn_s{seq_len}",
    )(q, k, k, v, v)


def _cexp(n, lam_re, lam_im):
    mag = jnp.exp(n * lam_re)
    return mag * jnp.cos(n * lam_im), mag * jnp.sin(n * lam_im)


def _ssm_kernel(u_ref, colp_ref, rowp_ref, dcol_ref, *rest, chunks_per_seq):
    n_w = (len(rest) - 3) // 2
    w_refs, y_ref, w_bf_refs, (a1_scr, cw_scr) = rest[:n_w], rest[n_w], rest[n_w + 1:2 * n_w + 1], rest[-2:]
    for gi in range(SSM_GROUPS_PER_STEP):
        rows = pl.ds(gi * SSM_GROUP, SSM_GROUP)
        _ssm_group(u_ref.at[:, rows], colp_ref.at[gi], rowp_ref.at[gi], dcol_ref.at[gi], y_ref.at[:, rows],
                   a1_scr.at[gi], cw_scr.at[gi], chunks_per_seq)
    for w_ref, w_bf_ref in zip(w_refs, w_bf_refs):
        w_bf_ref[...] = w_ref[...].astype(BF16)


def _ssm_group(u_ref, colp_ref, rowp_ref, dcol_ref, y_ref, a1_scr, cw_scr, chunks_per_seq):
    T, P, C = SSM_CHUNK, STATE_DIM, SSM_GROUP
    n_rows = C * T
    n_batch = u_ref.shape[0]
    n_cols = n_batch * chunks_per_seq
    lane = lax.broadcasted_iota(jnp.int32, (1, LANES), 1)
    low = lane < P

    a_re = colp_ref[:, 0:1]
    a_im = colp_ref[:, 1:2]
    dt = jnp.exp(colp_ref[:, 34:35])
    lam_re, lam_im = dt * a_re, dt * a_im
    ab_re, ab_im = _cexp(1.0, lam_re, lam_im)
    inv_a2 = 1.0 / (a_re * a_re + a_im * a_im)
    nr, ni = ab_re - 1.0, ab_im
    cf_re = (nr * a_re + ni * a_im) * inv_a2
    cf_im = (ni * a_re - nr * a_im) * inv_a2

    rev = (T - 1 - (lane & (T - 1))).astype(F32)
    pw_re, pw_im = _cexp(rev, lam_re, lam_im)
    g_re, g_im = [], []
    for k in range(C // 2):
        b_re = jnp.where(low, colp_ref[:, 2 + 2 * k:3 + 2 * k], colp_ref[:, 3 + 2 * k:4 + 2 * k])
        b_im = jnp.where(low, colp_ref[:, 18 + 2 * k:19 + 2 * k], colp_ref[:, 19 + 2 * k:20 + 2 * k])
        bb_re = cf_re * b_re - cf_im * b_im
        bb_im = cf_re * b_im + cf_im * b_re
        g_re.append(pw_re * bb_re - pw_im * bb_im)
        g_im.append(pw_re * bb_im + pw_im * bb_re)
    gm = jnp.concatenate([jnp.concatenate(g_re, axis=1), jnp.concatenate(g_im, axis=1)], axis=0)
    a1_scr[n_rows:, :] = gm.astype(BF16)

    c_re2 = rowp_ref[0:C, :]
    c_im2 = rowp_ref[C:2 * C, :]
    kr = jnp.dot(jnp.where(low, c_re2, -c_im2), gm, precision=lax.Precision.HIGHEST,
                 preferred_element_type=F32)
    krow = lax.broadcasted_iota(jnp.int32, (C, C * T), 0)
    klane = lax.broadcasted_iota(jnp.int32, (C, C * T), 1)
    kr = kr + jnp.where(klane == krow * T + (T - 1), dcol_ref[:, 0:1], 0.0)

    t_idx = lax.broadcasted_iota(jnp.int32, (T, LANES), 0)
    s_idx = lax.broadcasted_iota(jnp.int32, (T, LANES), 1) & (T - 1)
    causal = s_idx <= t_idx
    for c in range(C):
        for k in range(C * T // LANES):
            z = jnp.broadcast_to(kr[c:c + 1, k * LANES:(k + 1) * LANES], (T, LANES))
            z = pltpu.roll(z, LANES - (T - 1), axis=1, stride=1, stride_axis=0)
            a1_scr[c * T:(c + 1) * T, k * LANES:(k + 1) * LANES] = jnp.where(causal, z, 0.0).astype(BF16)

    dt_r = jnp.exp(rowp_ref[34:35, :])
    steps = (lax.broadcasted_iota(jnp.int32, (T, LANES), 0) + 1).astype(F32)
    pr, pi = _cexp(steps, dt_r * rowp_ref[32:33, :], dt_r * rowp_ref[33:34, :])
    for c in range(C):
        cr, ci = c_re2[c:c + 1, :], c_im2[c:c + 1, :]
        cw_scr[c * T:(c + 1) * T, :] = jnp.where(low, cr * pr - ci * pi, -(cr * pi + ci * pr)).astype(BF16)

    u = jnp.concatenate([u_ref[b].reshape(n_rows, chunks_per_seq) for b in range(n_batch)], axis=1)
    y1 = jnp.dot(a1_scr[...], u, preferred_element_type=F32)
    xr = y1[n_rows:n_rows + P]
    xi = y1[n_rows + P:]
    kidx = lax.broadcasted_iota(jnp.int32, (P, n_cols), 1) & (chunks_per_seq - 1)
    mr, mi = _cexp(float(T), lam_re, lam_im)
    sh = 1
    while sh < chunks_per_seq:
        rr = pltpu.roll(xr, sh, axis=1)
        ri = pltpu.roll(xi, sh, axis=1)
        ok = kidx >= sh
        xr, xi = (xr + jnp.where(ok, mr * rr - mi * ri, 0.0),
                  xi + jnp.where(ok, mr * ri + mi * rr, 0.0))
        mr, mi = mr * mr - mi * mi, 2.0 * mr * mi
        sh *= 2
    ok = kidx >= 1
    hr = jnp.where(ok, pltpu.roll(xr, 1, axis=1), 0.0)
    hi = jnp.where(ok, pltpu.roll(xi, 1, axis=1), 0.0)
    h = jnp.concatenate([hr, hi], axis=0).astype(BF16)
    y = y1[:n_rows] + jnp.dot(cw_scr[...], h, preferred_element_type=F32)
    z = jax.nn.gelu(y).astype(BF16)
    for b in range(n_batch):
        y_ref[b] = z[:, b * chunks_per_seq:(b + 1) * chunks_per_seq].reshape(C, T, chunks_per_seq)


def _ssm(ut, colp, rowp, dcol, f32_weights):
    batch, _, _, chunks_per_seq = ut.shape
    n_rows = SSM_GROUP * SSM_CHUNK
    per_step = SSM_GROUPS_PER_STEP
    steps = N_SSM_GROUPS // per_step
    blk = pl.BlockSpec((batch, per_step * SSM_GROUP, SSM_CHUNK, chunks_per_seq), lambda g: (0, g, 0, 0))
    per_group = lambda a: pl.BlockSpec((per_step,) + a.shape[1:], lambda g: (g, 0, 0))
    slab = lambda w: pl.BlockSpec((w.shape[0] // steps, w.shape[1]), lambda g: (g, 0))
    return pl.pallas_call(
        functools.partial(_ssm_kernel, chunks_per_seq=chunks_per_seq),
        grid=(steps,),
        in_specs=[blk, per_group(colp), per_group(rowp), per_group(dcol)] + [slab(w) for w in f32_weights],
        out_specs=[blk] + [slab(w) for w in f32_weights],
        out_shape=[jax.ShapeDtypeStruct(ut.shape, BF16)]
        + [jax.ShapeDtypeStruct(w.shape, BF16) for w in f32_weights],
        scratch_shapes=[pltpu.VMEM((per_step, n_rows + 2 * STATE_DIM, n_rows), BF16),
                        pltpu.VMEM((per_step, n_rows, 2 * STATE_DIM), BF16)],
        compiler_params=pltpu.CompilerParams(dimension_semantics=("arbitrary",)),
        name="ssm",
    )(ut, colp, rowp, dcol, *f32_weights)


def _pack_ssm_params(a_re, a_im, log_dt, b_re, b_im, c_re, c_im, d_skip):
    G, P, C = N_SSM_GROUPS, STATE_DIM, SSM_GROUP
    f = lambda t: t.astype(F32)
    log_dt_col = jnp.broadcast_to(f(log_dt)[:, None, None], (G, P, 1))
    colp = jnp.concatenate([f(a_re)[:, :, None], f(a_im)[:, :, None], f(b_re), f(b_im), log_dt_col,
                            jnp.zeros((G, P, LANES - 3 - 2 * C), F32)], axis=2)
    twice = lambda t: jnp.concatenate([f(t), f(t)], axis=-1)
    rowp = jnp.concatenate([twice(c_re), twice(c_im), twice(a_re)[:, None, :], twice(a_im)[:, None, :],
                            jnp.broadcast_to(f(log_dt)[:, None, None], (G, 1, 2 * P)),
                            jnp.zeros((G, 5, 2 * P), F32)], axis=1)
    dcol = jnp.broadcast_to(f(d_skip).reshape(G, C, 1), (G, C, LANES))
    return colp, rowp, dcol


def _mix_ffn2_kernel(x1_ref, o1_ref, o4_ref, o16_ref, l1_ref, l4_ref, l16_ref, z_ref,
                     wglu_ref, bglu_ref, wout_ref, gpost_ref, g3_ref, win_ref, wo_ref, p3_ref,
                     y_ref, o_scr, l_scr, x2_scr):
    tm = TOKEN_TILE
    step = pl.program_id(0)

    def ffn_previous_tile():
        prev_slot = lax.rem(step + 1, 2)
        h3 = _rms(x2_scr[prev_slot], g3_ref[...]).astype(BF16)
        y_ref[...] = _swiglu_ffn(lambda: x2_scr[prev_slot], h3, win_ref, wo_ref, p3_ref[...])

    def mix_tile():
        for n, (d, o_ref, l_ref) in enumerate(((4, o4_ref, l4_ref), (16, o16_ref, l16_ref))):
            for r in range(d):
                for c in range(ATTN_WIDTH // LANES):
                    o_scr[n, c, pl.ds(r, tm // d, stride=d), :] = (
                        o_ref[0, r, :, c * LANES:(c + 1) * LANES].astype(F32))
                for c in range(STATS_WIDTH // LANES):
                    l_scr[n, c, pl.ds(r, tm // d, stride=d), :] = l_ref[0, r, :, c * LANES:(c + 1) * LANES]
        ms = (l1_ref[...], l_scr[0, 0], l_scr[1, 0])
        ls = [pltpu.roll(st, LANES - L_LANE_OFFSET, axis=1) for st in ms]
        m = jnp.maximum(jnp.maximum(ms[0], ms[1]), ms[2])
        es = [jnp.exp2(mn - m) for mn in ms]
        inv = 1.0 / (es[0] * ls[0] + es[1] * ls[1] + es[2] * ls[2])
        ws = [e * inv for e in es]
        low_half = lax.broadcasted_iota(jnp.int32, (tm, 2 * HEAD_DIM), 1) < HEAD_DIM
        pairs = []
        for hp in range(N_HEADS // 2):
            cs = slice(2 * HEAD_DIM * hp, 2 * HEAD_DIM * (hp + 1))
            os_ = (o1_ref[:, cs].astype(F32), o_scr[0, hp], o_scr[1, hp])
            acc = jnp.zeros((tm, 2 * HEAD_DIM), F32)
            for w, o in zip(ws, os_):
                wexp = jnp.where(low_half, w[:, hp:hp + 1], w[:, HEAD_DIM + hp:HEAD_DIM + hp + 1])
                acc = acc + wexp * o
            pairs.append(acc)
        attn = jnp.concatenate(pairs, axis=-1).astype(BF16)
        z = z_ref[...]
        gate = jax.nn.sigmoid(jnp.dot(z, wglu_ref[...], preferred_element_type=F32) + bglu_ref[...])
        ssm = (z.astype(F32) * gate).astype(BF16)
        mixed = (jnp.dot(attn, wout_ref[0:ATTN_WIDTH, :], preferred_element_type=F32)
                 + jnp.dot(ssm, wout_ref[ATTN_WIDTH:, :], preferred_element_type=F32))
        x2_scr[lax.rem(step, 2)] = x1_ref[...] + _rms(mixed, gpost_ref[...])

    _run_pipelined(ffn_previous_tile, mix_tile)


def _mix_ffn2(x1, o1, o4, o16, l1, l4, l16, z, wglu, bglu, wout, gpost, g3, win, wo, p3, seq_len):
    rows = x1.shape[0]
    tm = TOKEN_TILE
    n_tiles = rows // tm
    tok_now, res_now = _token_specs(n_tiles, seq_len // tm, 0)
    tok_prev, _ = _token_specs(n_tiles, seq_len // tm, 1)
    return pl.pallas_call(
        _mix_ffn2_kernel,
        grid=(n_tiles + 1,),
        in_specs=[tok_now(D_MODEL), tok_now(ATTN_WIDTH), res_now(4, ATTN_WIDTH), res_now(16, ATTN_WIDTH),
                  tok_now(STATS_WIDTH), res_now(4, STATS_WIDTH), res_now(16, STATS_WIDTH), tok_now(SSM_WIDTH),
                  _const_spec((SSM_WIDTH, SSM_WIDTH)), _const_spec((1, SSM_WIDTH)),
                  _const_spec((D_MODEL, D_MODEL)), _const_spec((1, D_MODEL)), _const_spec((1, D_MODEL)),
                  _const_spec((D_MODEL, 2 * D_FF)), _const_spec((D_FF, D_MODEL)),
                  _const_spec((1, D_MODEL))],
        out_specs=tok_prev(D_MODEL),
        out_shape=jax.ShapeDtypeStruct((rows, D_MODEL), F32),
        scratch_shapes=[pltpu.VMEM((2, ATTN_WIDTH // LANES, tm, LANES), F32),
                        pltpu.VMEM((2, STATS_WIDTH // LANES, tm, LANES), F32),
                        pltpu.VMEM((2, tm, D_MODEL), F32)],
        compiler_params=pltpu.CompilerParams(
            dimension_semantics=("arbitrary",), vmem_limit_bytes=VMEM_LIMIT_BYTES),
        name="mix_ffn2",
    )(x1, o1, o4, o16, l1, l4, l16, z, wglu, bglu, wout, gpost, g3, win, wo, p3)


def _row(v):
    return v.astype(F32).reshape(1, -1)


def kernel(x, ffn1_pre_g, ffn1_w_in, ffn1_w_out, ffn1_post_g, mix_pre_g, w_mix_in, a_re, a_im, log_dt, b_re, b_im, c_re, c_im, d_skip, w_glu, b_glu, w_mix_out, mix_post_g, ffn2_pre_g, ffn2_w_in, ffn2_w_out, ffn2_post_g):
    B, S, _ = x.shape
    depth = ffn1_pre_g.shape[0]
    n_chunks = S // SSM_CHUNK
    for l in range(depth):
        (x1, q1, k1, v1, q4, k4, v4, q16, k16, v16, u) = _ffn1_proj(
            x, _row(ffn1_pre_g[l]), ffn1_w_in[l].astype(BF16), ffn1_w_out[l].astype(BF16),
            _row(ffn1_post_g[l]), _row(mix_pre_g[l]), w_mix_in[l].astype(BF16))

        outs = []
        for d, (q, k, v) in zip(DILATIONS, ((q1, k1, v1), (q4, k4, v4), (q16, k16, v16))):
            flat = lambda t: t.reshape(B * S, ATTN_WIDTH)
            outs.append(_attention_branch(flat(q), flat(k), flat(v), S // d, d))
        (o1, l1), (o4, l4), (o16, l16) = outs
        o4 = o4.reshape(B, 4, S // 4, ATTN_WIDTH)
        l4 = l4.reshape(B, 4, S // 4, STATS_WIDTH)
        o16 = o16.reshape(B, 16, S // 16, ATTN_WIDTH)
        l16 = l16.reshape(B, 16, S // 16, STATS_WIDTH)

        colp, rowp, dcol = _pack_ssm_params(a_re[l], a_im[l], log_dt[l], b_re[l], b_im[l], c_re[l], c_im[l],
                                            d_skip[l])
        ut = u.reshape(B, n_chunks, SSM_CHUNK, SSM_WIDTH).transpose(0, 3, 2, 1)
        f32 = lambda w: w.astype(F32)
        zt, w_glu_bf, w_mix_out_bf, w_in2_bf, w_out2_bf = _ssm(
            ut, colp, rowp, dcol, (f32(w_glu[l]), f32(w_mix_out[l]), f32(ffn2_w_in[l]), f32(ffn2_w_out[l])))
        z = zt.transpose(0, 3, 2, 1).reshape(B * S, SSM_WIDTH)

        x = _mix_ffn2(x1, o1, o4, o16, l1, l4, l16, z, w_glu_bf, _row(b_glu[l]),
                      w_mix_out_bf, _row(mix_post_g[l]), _row(ffn2_pre_g[l]),
                      w_in2_bf, w_out2_bf, _row(ffn2_post_g[l]), S).reshape(B, S, D_MODEL)
    return x
```

```python
import functools

import jax
import jax.numpy as jnp
from jax import lax
from jax.experimental import pallas as pl
from jax.experimental.pallas import tpu as pltpu

F32 = jnp.float32
BF16 = jnp.bfloat16

D_MODEL = 1024
ATTN_WIDTH = 512
SSM_WIDTH = 512
HEAD_DIM = 64
N_HEADS = 8
DILATIONS = (1, 4, 16)
WINDOW_STEPS = 128
QBLK = 128
SSM_GROUP = 16
N_SSM_GROUPS = 32
STATE_DIM = 64
D_FF = 2816
NORM_EPS = 1e-6

TOKEN_TILE = 512
FF_CHUNKS = (256,) * 11
SSM_CHUNK = 64
SSM_GROUPS_PER_STEP = 2
ATTN_TILE = 2048
MASK_VALUE = float("-inf")
LANES = 128
LOG2_E = 1.4426950408889634
QK_SCALE_LOG2 = HEAD_DIM ** -0.5 * LOG2_E
STATS_WIDTH = 2 * HEAD_DIM
L_LANE_OFFSET = 16
PROJ_SLABS = 4 * ATTN_WIDTH // LANES
VMEM_LIMIT_BYTES = 56 * 1024 * 1024


def _rms(x, g):
    return x * lax.rsqrt(jnp.mean(x * x, axis=-1, keepdims=True) + NORM_EPS) * g


def _swiglu_chunks(h, acc, chunks, wg_ref, wu_ref, wo_ref):
    for start, width in chunks:
        sl = slice(start, start + width)
        gate = jnp.dot(h, wg_ref[:, sl], preferred_element_type=F32)
        up = jnp.dot(h, wu_ref[:, sl], preferred_element_type=F32)
        act = (gate * jax.nn.sigmoid(gate) * up).astype(BF16)
        acc = acc + jnp.dot(act, wo_ref[sl, :], preferred_element_type=F32)
    return acc


def _ff_chunks():
    starts = [sum(FF_CHUNKS[:j]) for j in range(len(FF_CHUNKS))]
    return list(zip(starts, FF_CHUNKS))


def _swiglu_ffn(read_x, h, win_ref, wo_ref, post_g):
    wg_ref, wu_ref = win_ref.at[:, 0:D_FF], win_ref.at[:, D_FF:2 * D_FF]
    acc = _swiglu_chunks(h, jnp.zeros((h.shape[0], D_MODEL), F32), _ff_chunks(), wg_ref, wu_ref, wo_ref)
    return read_x() + 0.5 * _rms(acc, post_g)


def _const_spec(shape):
    return pl.BlockSpec(shape, lambda *_: (0,) * len(shape), pipeline_mode=pl.Buffered(1))


def _token_specs(n_tiles, tiles_per_seq, lag):
    tm = TOKEN_TILE

    def tile(i):
        return jnp.clip(i - lag, 0, n_tiles - 1)

    def tok(w):
        return pl.BlockSpec((tm, w), lambda i: (tile(i), 0))

    def res(d, w):
        return pl.BlockSpec((1, d, tm // d, w),
                            lambda i: (tile(i) // tiles_per_seq, 0, tile(i) % tiles_per_seq, 0))

    return tok, res


def _run_pipelined(lagging, leading):
    step = pl.program_id(0)
    last = pl.num_programs(0) - 1

    @pl.when(step == 0)
    def _first():
        leading()

    @pl.when(jnp.logical_and(step > 0, step < last))
    def _interior():
        lagging()
        leading()

    @pl.when(step == last)
    def _last():
        lagging()


def _ffn1_proj_kernel(x_ref, g1_ref, win_ref, wo_ref, p1_ref, gm_ref, wm_ref,
                      x1_ref, qkv1_ref, qkv4_ref, qkv16_ref, u_ref, proj_scr, mod4_scr):
    lanes_per = ATTN_WIDTH // LANES

    def write_previous_tile():
        q4_rows = TOKEN_TILE // 4
        for slab in range(3 * lanes_per):
            cs = slice(slab * LANES, (slab + 1) * LANES)
            qkv1_ref[:, cs] = proj_scr[slab].astype(BF16)
            for b in range(4):
                cls4 = proj_scr[slab, pl.ds(b, q4_rows, stride=4), :]
                qkv4_ref[0, b, :, cs] = cls4.astype(BF16)
                mod4_scr[slab, b * q4_rows:(b + 1) * q4_rows, :] = cls4
            for b in range(4):
                for a in range(4):
                    qkv16_ref[0, 4 * a + b, :, cs] = mod4_scr[
                        slab, pl.ds(b * q4_rows + a, q4_rows // 4, stride=4), :].astype(BF16)
        for c in range(lanes_per):
            u_ref[:, c * LANES:(c + 1) * LANES] = proj_scr[3 * lanes_per + c].astype(BF16)

    def compute_tile():
        h1 = _rms(x_ref[...], g1_ref[...]).astype(BF16)
        x1 = _swiglu_ffn(lambda: x_ref[...], h1, win_ref, wo_ref, p1_ref[...])
        x1_ref[...] = x1
        h = _rms(x1, gm_ref[...]).astype(BF16)
        proj = jnp.dot(h, wm_ref[...], preferred_element_type=F32)
        for cb in range(PROJ_SLABS):
            slab = proj[:, cb * LANES:(cb + 1) * LANES]
            proj_scr[cb] = slab * QK_SCALE_LOG2 if cb < lanes_per else slab

    _run_pipelined(write_previous_tile, compute_tile)


def _ffn1_proj(x, g1, win, wo, p1, gm, wm):
    B, S, _ = x.shape
    tm = TOKEN_TILE
    n_tiles = B * S // tm
    tok_now, _ = _token_specs(n_tiles, S // tm, 0)
    tok_prev, res_prev = _token_specs(n_tiles, S // tm, 1)
    qkv_w = 3 * ATTN_WIDTH
    nat = jax.ShapeDtypeStruct((B * S, qkv_w), BF16)
    r4 = jax.ShapeDtypeStruct((B, 4, S // 4, qkv_w), BF16)
    r16 = jax.ShapeDtypeStruct((B, 16, S // 16, qkv_w), BF16)
    return pl.pallas_call(
        _ffn1_proj_kernel,
        grid=(n_tiles + 1,),
        in_specs=[tok_now(D_MODEL), _const_spec((1, D_MODEL)),
                  _const_spec((D_MODEL, 2 * D_FF)), _const_spec((D_FF, D_MODEL)),
                  _const_spec((1, D_MODEL)), _const_spec((1, D_MODEL)), _const_spec((D_MODEL, 4 * ATTN_WIDTH))],
        out_specs=[tok_now(D_MODEL), tok_prev(qkv_w), res_prev(4, qkv_w), res_prev(16, qkv_w),
                   tok_prev(SSM_WIDTH)],
        out_shape=[jax.ShapeDtypeStruct((B * S, D_MODEL), F32), nat, r4, r16,
                   jax.ShapeDtypeStruct((B * S, SSM_WIDTH), BF16)],
        scratch_shapes=[pltpu.VMEM((PROJ_SLABS, tm, LANES), F32),
                        pltpu.VMEM((3 * ATTN_WIDTH // LANES, tm, LANES), F32)],
        compiler_params=pltpu.CompilerParams(
            dimension_semantics=("arbitrary",), vmem_limit_bytes=VMEM_LIMIT_BYTES),
        name="ffn1_proj",
    )(x.reshape(B * S, D_MODEL), g1, win, wo, p1, gm, wm)


def _attn_kernel(q_ref, kc_ref, kp_ref, vc_ref, vp_ref, o_ref, st_ref, bias_ref, *, blocks_per_seq, dilation):
    @pl.when(pl.program_id(0) == 0)
    def _build_bias_tables():
        qi = lax.broadcasted_iota(jnp.int32, (QBLK, 2 * QBLK), 0)
        ci = lax.broadcasted_iota(jnp.int32, (QBLK, 2 * QBLK), 1)
        steps = QBLK + qi - ci
        dist = (steps * dilation).astype(F32)
        band = jnp.where(steps >= 0, jnp.where(steps <= WINDOW_STEPS, 1, 0), 0)
        band_first = jnp.where(ci >= QBLK, band, 0)
        for h in range(N_HEADS):
            bias = -(2.0 ** (-8.0 * (h + 1) / N_HEADS) * LOG2_E) * dist
            bias_ref[0, h] = jnp.where(band == 1, bias, MASK_VALUE)
            bias_ref[1, h] = jnp.where(band_first == 1, bias, MASK_VALUE)

    n_blocks = ATTN_TILE // QBLK
    lane = lax.broadcasted_iota(jnp.int32, (1, 2 * HEAD_DIM), 1)
    head_mask = ((lane < HEAD_DIM).astype(BF16), (lane >= HEAD_DIM).astype(BF16))
    ones_cols = tuple(jnp.broadcast_to(mk, (2 * QBLK, 2 * HEAD_DIM)) for mk in head_mask)
    lane_f = lax.broadcasted_iota(jnp.int32, (QBLK, 2 * HEAD_DIM), 1)
    low_half = lane_f < HEAD_DIM

    for j in range(n_blocks):
        rows = slice(j * QBLK, (j + 1) * QBLK)
        if blocks_per_seq >= n_blocks:
            tiles_per_seq = blocks_per_seq // n_blocks
            table = (lax.rem(pl.program_id(0), tiles_per_seq) == 0).astype(jnp.int32) if j == 0 else 0
        else:
            table = 1 if j % blocks_per_seq == 0 else 0
        st_tile = jnp.ones((QBLK, 2 * HEAD_DIM), F32)
        for hp in range(N_HEADS // 2):
            cs = slice(2 * HEAD_DIM * hp, 2 * HEAD_DIM * (hp + 1))
            q = q_ref[rows, cs]
            if j == 0:
                kk = jnp.concatenate([kp_ref[:, cs], kc_ref[0:QBLK, cs]], axis=0)
                vv = jnp.concatenate([vp_ref[:, cs], vc_ref[0:QBLK, cs]], axis=0)
            else:
                kk = kc_ref[(j - 1) * QBLK:(j + 1) * QBLK, cs]
                vv = vc_ref[(j - 1) * QBLK:(j + 1) * QBLK, cs]
            ps, ms, vas = [], [], []
            for e in range(2):
                s = lax.dot_general(q * head_mask[e], kk, (((1,), (1,)), ((), ())),
                                    preferred_element_type=F32)
                s = s + bias_ref[table, 2 * hp + e]
                m = jnp.max(s, axis=-1, keepdims=True)
                ps.append(jnp.exp2(s - m).astype(BF16))
                ms.append(m)
                vas.append(jnp.concatenate([vv * head_mask[e], ones_cols[e]], axis=1))
            pv = jnp.dot(jnp.concatenate(ps, axis=1), jnp.concatenate(vas, axis=0),
                         preferred_element_type=F32)
            o_ref[rows, cs] = pv[:, :2 * HEAD_DIM].astype(BF16)
            in_half = lane_f & (HEAD_DIM - 1)
            st_tile = jnp.where(in_half == hp, jnp.where(low_half, ms[0], ms[1]),
                                jnp.where(in_half == hp + L_LANE_OFFSET, pv[:, 2 * HEAD_DIM:], st_tile))
        st_ref[rows, :] = st_tile


def _attention_branch(qkv, seq_len, dilation):
    rows = qkv.shape[0]
    ratio = ATTN_TILE // QBLK
    cur = lambda part: pl.BlockSpec((ATTN_TILE, ATTN_WIDTH), lambda i: (i, part))
    prev = lambda part: pl.BlockSpec((QBLK, ATTN_WIDTH), lambda i: (jnp.maximum(i * ratio - 1, 0), part))
    q, k, v = qkv, qkv, qkv
    return pl.pallas_call(
        functools.partial(_attn_kernel, blocks_per_seq=seq_len // QBLK, dilation=dilation),
        grid=(rows // ATTN_TILE,),
        in_specs=[cur(0), cur(1), prev(1), cur(2), prev(2)],
        out_specs=[cur(0), pl.BlockSpec((ATTN_TILE, STATS_WIDTH), lambda i: (i, 0))],
        out_shape=[jax.ShapeDtypeStruct((rows, ATTN_WIDTH), BF16),
                   jax.ShapeDtypeStruct((rows, STATS_WIDTH), F32)],
        scratch_shapes=[pltpu.VMEM((2, N_HEADS, QBLK, 2 * QBLK), F32)],
        compiler_params=pltpu.CompilerParams(dimension_semantics=("arbitrary",)),
        name=f"attn_s{seq_len}",
    )(q, k, k, v, v)


def _cexp(n, lam_re, lam_im):
    mag = jnp.exp(n * lam_re)
    return mag * jnp.cos(n * lam_im), mag * jnp.sin(n * lam_im)


def _ssm_kernel(u_ref, colp_ref, rowp_ref, dcol_ref, *rest, chunks_per_seq):
    n_w = (len(rest) - 3) // 2
    w_refs, y_ref, w_bf_refs, (a1_scr, cw_scr) = rest[:n_w], rest[n_w], rest[n_w + 1:2 * n_w + 1], rest[-2:]
    for gi in range(SSM_GROUPS_PER_STEP):
        rows = pl.ds(gi * SSM_GROUP, SSM_GROUP)
        _ssm_group(u_ref.at[:, rows], colp_ref.at[gi], rowp_ref.at[gi], dcol_ref.at[gi], y_ref.at[:, rows],
                   a1_scr.at[gi], cw_scr.at[gi], chunks_per_seq)
    for w_ref, w_bf_ref in zip(w_refs, w_bf_refs):
        w_bf_ref[...] = w_ref[...].astype(BF16)


def _ssm_group(u_ref, colp_ref, rowp_ref, dcol_ref, y_ref, a1_scr, cw_scr, chunks_per_seq):
    T, P, C = SSM_CHUNK, STATE_DIM, SSM_GROUP
    n_rows = C * T
    n_batch = u_ref.shape[0]
    n_cols = n_batch * chunks_per_seq
    lane = lax.broadcasted_iota(jnp.int32, (1, LANES), 1)
    low = lane < P

    a_re = colp_ref[:, 0:1]
    a_im = colp_ref[:, 1:2]
    dt = jnp.exp(colp_ref[:, 34:35])
    lam_re, lam_im = dt * a_re, dt * a_im
    ab_re, ab_im = _cexp(1.0, lam_re, lam_im)
    inv_a2 = 1.0 / (a_re * a_re + a_im * a_im)
    nr, ni = ab_re - 1.0, ab_im
    cf_re = (nr * a_re + ni * a_im) * inv_a2
    cf_im = (ni * a_re - nr * a_im) * inv_a2

    rev = (T - 1 - (lane & (T - 1))).astype(F32)
    pw_re, pw_im = _cexp(rev, lam_re, lam_im)
    g_re, g_im = [], []
    for k in range(C // 2):
        b_re = jnp.where(low, colp_ref[:, 2 + 2 * k:3 + 2 * k], colp_ref[:, 3 + 2 * k:4 + 2 * k])
        b_im = jnp.where(low, colp_ref[:, 18 + 2 * k:19 + 2 * k], colp_ref[:, 19 + 2 * k:20 + 2 * k])
        bb_re = cf_re * b_re - cf_im * b_im
        bb_im = cf_re * b_im + cf_im * b_re
        g_re.append(pw_re * bb_re - pw_im * bb_im)
        g_im.append(pw_re * bb_im + pw_im * bb_re)
    gm = jnp.concatenate([jnp.concatenate(g_re, axis=1), jnp.concatenate(g_im, axis=1)], axis=0)
    a1_scr[n_rows:, :] = gm.astype(BF16)

    c_re2 = rowp_ref[0:C, :]
    c_im2 = rowp_ref[C:2 * C, :]
    kr = jnp.dot(jnp.where(low, c_re2, -c_im2), gm, precision=lax.Precision.HIGHEST,
                 preferred_element_type=F32)
    krow = lax.broadcasted_iota(jnp.int32, (C, C * T), 0)
    klane = lax.broadcasted_iota(jnp.int32, (C, C * T), 1)
    kr = kr + jnp.where(klane == krow * T + (T - 1), dcol_ref[:, 0:1], 0.0)

    t_idx = lax.broadcasted_iota(jnp.int32, (T, LANES), 0)
    s_idx = lax.broadcasted_iota(jnp.int32, (T, LANES), 1) & (T - 1)
    causal = s_idx <= t_idx
    for c in range(C):
        for k in range(C * T // LANES):
            z = jnp.broadcast_to(kr[c:c + 1, k * LANES:(k + 1) * LANES], (T, LANES))
            z = pltpu.roll(z, LANES - (T - 1), axis=1, stride=1, stride_axis=0)
            a1_scr[c * T:(c + 1) * T, k * LANES:(k + 1) * LANES] = jnp.where(causal, z, 0.0).astype(BF16)

    dt_r = jnp.exp(rowp_ref[34:35, :])
    steps = (lax.broadcasted_iota(jnp.int32, (T, LANES), 0) + 1).astype(F32)
    pr, pi = _cexp(steps, dt_r * rowp_ref[32:33, :], dt_r * rowp_ref[33:34, :])
    for c in range(C):
        cr, ci = c_re2[c:c + 1, :], c_im2[c:c + 1, :]
        cw_scr[c * T:(c + 1) * T, :] = jnp.where(low, cr * pr - ci * pi, -(cr * pi + ci * pr)).astype(BF16)

    u = jnp.concatenate([u_ref[b].reshape(n_rows, chunks_per_seq) for b in range(n_batch)], axis=1)
    y1 = jnp.dot(a1_scr[...], u, preferred_element_type=F32)
    xr = y1[n_rows:n_rows + P]
    xi = y1[n_rows + P:]
    kidx = lax.broadcasted_iota(jnp.int32, (P, n_cols), 1) & (chunks_per_seq - 1)
    mr, mi = _cexp(float(T), lam_re, lam_im)
    sh = 1
    while sh < chunks_per_seq:
        rr = pltpu.roll(xr, sh, axis=1)
        ri = pltpu.roll(xi, sh, axis=1)
        ok = kidx >= sh
        xr, xi = (xr + jnp.where(ok, mr * rr - mi * ri, 0.0),
                  xi + jnp.where(ok, mr * ri + mi * rr, 0.0))
        mr, mi = mr * mr - mi * mi, 2.0 * mr * mi
        sh *= 2
    ok = kidx >= 1
    hr = jnp.where(ok, pltpu.roll(xr, 1, axis=1), 0.0)
    hi = jnp.where(ok, pltpu.roll(xi, 1, axis=1), 0.0)
    h = jnp.concatenate([hr, hi], axis=0).astype(BF16)
    y = y1[:n_rows] + jnp.dot(cw_scr[...], h, preferred_element_type=F32)
    z = jax.nn.gelu(y).astype(BF16)
    for b in range(n_batch):
        y_ref[b] = z[:, b * chunks_per_seq:(b + 1) * chunks_per_seq].reshape(C, T, chunks_per_seq)


def _ssm(ut, colp, rowp, dcol, f32_weights):
    batch, _, _, chunks_per_seq = ut.shape
    n_rows = SSM_GROUP * SSM_CHUNK
    per_step = SSM_GROUPS_PER_STEP
    steps = N_SSM_GROUPS // per_step
    blk = pl.BlockSpec((batch, per_step * SSM_GROUP, SSM_CHUNK, chunks_per_seq), lambda g: (0, g, 0, 0))
    per_group = lambda a: pl.BlockSpec((per_step,) + a.shape[1:], lambda g: (g, 0, 0))
    slab = lambda w: pl.BlockSpec((w.shape[0] // steps, w.shape[1]), lambda g: (g, 0))
    return pl.pallas_call(
        functools.partial(_ssm_kernel, chunks_per_seq=chunks_per_seq),
        grid=(steps,),
        in_specs=[blk, per_group(colp), per_group(rowp), per_group(dcol)] + [slab(w) for w in f32_weights],
        out_specs=[blk] + [slab(w) for w in f32_weights],
        out_shape=[jax.ShapeDtypeStruct(ut.shape, BF16)]
        + [jax.ShapeDtypeStruct(w.shape, BF16) for w in f32_weights],
        scratch_shapes=[pltpu.VMEM((per_step, n_rows + 2 * STATE_DIM, n_rows), BF16),
                        pltpu.VMEM((per_step, n_rows, 2 * STATE_DIM), BF16)],
        compiler_params=pltpu.CompilerParams(dimension_semantics=("arbitrary",)),
        name="ssm",
    )(ut, colp, rowp, dcol, *f32_weights)


def _pack_ssm_params(a_re, a_im, log_dt, b_re, b_im, c_re, c_im, d_skip):
    G, P, C = N_SSM_GROUPS, STATE_DIM, SSM_GROUP
    f = lambda t: t.astype(F32)
    log_dt_col = jnp.broadcast_to(f(log_dt)[:, None, None], (G, P, 1))
    colp = jnp.concatenate([f(a_re)[:, :, None], f(a_im)[:, :, None], f(b_re), f(b_im), log_dt_col,
                            jnp.zeros((G, P, LANES - 3 - 2 * C), F32)], axis=2)
    twice = lambda t: jnp.concatenate([f(t), f(t)], axis=-1)
    rowp = jnp.concatenate([twice(c_re), twice(c_im), twice(a_re)[:, None, :], twice(a_im)[:, None, :],
                            jnp.broadcast_to(f(log_dt)[:, None, None], (G, 1, 2 * P)),
                            jnp.zeros((G, 5, 2 * P), F32)], axis=1)
    dcol = jnp.broadcast_to(f(d_skip).reshape(G, C, 1), (G, C, LANES))
    return colp, rowp, dcol


def _mix_ffn2_kernel(x1_ref, o1_ref, o4_ref, o16_ref, l1_ref, l4_ref, l16_ref, z_ref,
                     wglu_ref, bglu_ref, wout_ref, gpost_ref, g3_ref, win_ref, wo_ref, p3_ref,
                     y_ref, o_scr, l_scr, x2_scr):
    tm = TOKEN_TILE
    step = pl.program_id(0)

    def ffn_previous_tile():
        prev_slot = lax.rem(step + 1, 2)
        h3 = _rms(x2_scr[prev_slot], g3_ref[...]).astype(BF16)
        y_ref[...] = _swiglu_ffn(lambda: x2_scr[prev_slot], h3, win_ref, wo_ref, p3_ref[...])

    def mix_tile():
        for n, (d, o_ref, l_ref) in enumerate(((4, o4_ref, l4_ref), (16, o16_ref, l16_ref))):
            for r in range(d):
                for c in range(ATTN_WIDTH // LANES):
                    o_scr[n, c, pl.ds(r, tm // d, stride=d), :] = (
                        o_ref[0, r, :, c * LANES:(c + 1) * LANES].astype(F32))
                for c in range(STATS_WIDTH // LANES):
                    l_scr[n, c, pl.ds(r, tm // d, stride=d), :] = l_ref[0, r, :, c * LANES:(c + 1) * LANES]
        ms = (l1_ref[...], l_scr[0, 0], l_scr[1, 0])
        ls = [pltpu.roll(st, LANES - L_LANE_OFFSET, axis=1) for st in ms]
        m = jnp.maximum(jnp.maximum(ms[0], ms[1]), ms[2])
        es = [jnp.exp2(mn - m) for mn in ms]
        inv = 1.0 / (es[0] * ls[0] + es[1] * ls[1] + es[2] * ls[2])
        ws = [e * inv for e in es]
        low_half = lax.broadcasted_iota(jnp.int32, (tm, 2 * HEAD_DIM), 1) < HEAD_DIM
        pairs = []
        for hp in range(N_HEADS // 2):
            cs = slice(2 * HEAD_DIM * hp, 2 * HEAD_DIM * (hp + 1))
            os_ = (o1_ref[:, cs].astype(F32), o_scr[0, hp], o_scr[1, hp])
            acc = jnp.zeros((tm, 2 * HEAD_DIM), F32)
            for w, o in zip(ws, os_):
                wexp = jnp.where(low_half, w[:, hp:hp + 1], w[:, HEAD_DIM + hp:HEAD_DIM + hp + 1])
                acc = acc + wexp * o
            pairs.append(acc)
        attn = jnp.concatenate(pairs, axis=-1).astype(BF16)
        z = z_ref[...]
        gate = jax.nn.sigmoid(jnp.dot(z, wglu_ref[...], preferred_element_type=F32) + bglu_ref[...])
        ssm = (z.astype(F32) * gate).astype(BF16)
        mixed = (jnp.dot(attn, wout_ref[0:ATTN_WIDTH, :], preferred_element_type=F32)
                 + jnp.dot(ssm, wout_ref[ATTN_WIDTH:, :], preferred_element_type=F32))
        x2_scr[lax.rem(step, 2)] = x1_ref[...] + _rms(mixed, gpost_ref[...])

    _run_pipelined(ffn_previous_tile, mix_tile)


def _mix_ffn2(x1, o1, o4, o16, l1, l4, l16, z, wglu, bglu, wout, gpost, g3, win, wo, p3, seq_len):
    rows = x1.shape[0]
    tm = TOKEN_TILE
    n_tiles = rows // tm
    tok_now, res_now = _token_specs(n_tiles, seq_len // tm, 0)
    tok_prev, _ = _token_specs(n_tiles, seq_len // tm, 1)
    return pl.pallas_call(
        _mix_ffn2_kernel,
        grid=(n_tiles + 1,),
        in_specs=[tok_now(D_MODEL), tok_now(ATTN_WIDTH), res_now(4, ATTN_WIDTH), res_now(16, ATTN_WIDTH),
                  tok_now(STATS_WIDTH), res_now(4, STATS_WIDTH), res_now(16, STATS_WIDTH), tok_now(SSM_WIDTH),
                  _const_spec((SSM_WIDTH, SSM_WIDTH)), _const_spec((1, SSM_WIDTH)),
                  _const_spec((D_MODEL, D_MODEL)), _const_spec((1, D_MODEL)), _const_spec((1, D_MODEL)),
                  _const_spec((D_MODEL, 2 * D_FF)), _const_spec((D_FF, D_MODEL)),
                  _const_spec((1, D_MODEL))],
        out_specs=tok_prev(D_MODEL),
        out_shape=jax.ShapeDtypeStruct((rows, D_MODEL), F32),
        scratch_shapes=[pltpu.VMEM((2, ATTN_WIDTH // LANES, tm, LANES), F32),
                        pltpu.VMEM((2, STATS_WIDTH // LANES, tm, LANES), F32),
                        pltpu.VMEM((2, tm, D_MODEL), F32)],
        compiler_params=pltpu.CompilerParams(
            dimension_semantics=("arbitrary",), vmem_limit_bytes=VMEM_LIMIT_BYTES),
        name="mix_ffn2",
    )(x1, o1, o4, o16, l1, l4, l16, z, wglu, bglu, wout, gpost, g3, win, wo, p3)


def _row(v):
    return v.astype(F32).reshape(1, -1)


def kernel(x, ffn1_pre_g, ffn1_w_in, ffn1_w_out, ffn1_post_g, mix_pre_g, w_mix_in, a_re, a_im, log_dt, b_re, b_im, c_re, c_im, d_skip, w_glu, b_glu, w_mix_out, mix_post_g, ffn2_pre_g, ffn2_w_in, ffn2_w_out, ffn2_post_g):
    B, S, _ = x.shape
    depth = ffn1_pre_g.shape[0]
    n_chunks = S // SSM_CHUNK
    for l in range(depth):
        x1, qkv1, qkv4, qkv16, u = _ffn1_proj(
            x, _row(ffn1_pre_g[l]), ffn1_w_in[l].astype(BF16), ffn1_w_out[l].astype(BF16),
            _row(ffn1_post_g[l]), _row(mix_pre_g[l]), w_mix_in[l].astype(BF16))

        outs = []
        for d, qkv in zip(DILATIONS, (qkv1, qkv4, qkv16)):
            outs.append(_attention_branch(qkv.reshape(B * S, 3 * ATTN_WIDTH), S // d, d))
        (o1, l1), (o4, l4), (o16, l16) = outs
        o4 = o4.reshape(B, 4, S // 4, ATTN_WIDTH)
        l4 = l4.reshape(B, 4, S // 4, STATS_WIDTH)
        o16 = o16.reshape(B, 16, S // 16, ATTN_WIDTH)
        l16 = l16.reshape(B, 16, S // 16, STATS_WIDTH)

        colp, rowp, dcol = _pack_ssm_params(a_re[l], a_im[l], log_dt[l], b_re[l], b_im[l], c_re[l], c_im[l],
                                            d_skip[l])
        ut = u.reshape(B, n_chunks, SSM_CHUNK, SSM_WIDTH).transpose(0, 3, 2, 1)
        f32 = lambda w: w.astype(F32)
        zt, w_glu_bf, w_mix_out_bf, w_in2_bf, w_out2_bf = _ssm(
            ut, colp, rowp, dcol, (f32(w_glu[l]), f32(w_mix_out[l]), f32(ffn2_w_in[l]), f32(ffn2_w_out[l])))
        z = zt.transpose(0, 3, 2, 1).reshape(B * S, SSM_WIDTH)

        x = _mix_ffn2(x1, o1, o4, o16, l1, l4, l16, z, w_glu_bf, _row(b_glu[l]),
                      w_mix_out_bf, _row(mix_post_g[l]), _row(ffn2_pre_g[l]),
                      w_in2_bf, w_out2_bf, _row(ffn2_post_g[l]), S).reshape(B, S, D_MODEL)
    return x
```
